```python
import math
import jax
import jax.numpy as jnp
from jax import lax
import numpy as np


D_MODEL = 2048
BATCH = 2
SEQ = 8192
DEPTH = 1

N_META = 16
NORM_EPS = 1e-5

SSM_WIDTH = D_MODEL // 2
SSM_GROUP = 16
SSM_GROUPS = SSM_WIDTH // SSM_GROUP
SSM_STATE = 64
DT_MIN = 1e-3
DT_MAX = 1e-1

ATTN_HEADS = 8
ATTN_HEAD_DIM = 64
ATTN_V_DIM = 2 * ATTN_HEAD_DIM
ATTN_QK_WIDTH = ATTN_HEADS * 2 * ATTN_HEAD_DIM
ATTN_V_WIDTH = ATTN_HEADS * ATTN_V_DIM
Q_BLOCK = 128
ALIBI_MAX_EXP = 8.0

N_EXPERTS = 32
TOP_K = 4
D_FF = D_MODEL
SWIGLU_LIMIT = 7.0
SWIGLU_ALPHA = 1.702
MOE_BLOCK = 128

IN_COLS = SSM_WIDTH + 2 * ATTN_QK_WIDTH + ATTN_V_WIDTH + 2 * D_MODEL

kernel_name = 'hybrid_s5_diffattn_moe_encoder'


def _rmsnorm(x, g):
    xf = x.astype(jnp.float32)
    y = xf * lax.rsqrt(jnp.mean(xf * xf, axis=-1, keepdims=True) + NORM_EPS)
    return (y * g.astype(jnp.float32)).astype(x.dtype)


def _complex_affine_combine(e1, e2):
    a1r, a1i, b1r, b1i = e1
    a2r, a2i, b2r, b2i = e2
    return (a2r * a1r - a2i * a1i,
            a2r * a1i + a2i * a1r,
            a2r * b1r - a2i * b1i + b2r,
            a2r * b1i + a2i * b1r + b2i)


def _s5_direction(u, lam_re, lam_im, log_dt, b_re, b_im, c_re, c_im, reverse):
    f32 = jnp.float32
    lam_re = lam_re.astype(f32)
    lam_im = lam_im.astype(f32)
    b_re = b_re.astype(f32)
    b_im = b_im.astype(f32)
    dt = jnp.exp(log_dt.astype(f32))[:, None]
    mag = jnp.exp(lam_re * dt)
    ar = mag * jnp.cos(lam_im * dt)
    ai = mag * jnp.sin(lam_im * dt)
    den = lam_re * lam_re + lam_im * lam_im
    fr = ((ar - 1.0) * lam_re + ai * lam_im) / den
    fi = (ai * lam_re - (ar - 1.0) * lam_im) / den
    bb_re = fr[..., None] * b_re - fi[..., None] * b_im
    bb_im = fr[..., None] * b_im + fi[..., None] * b_re
    drive_re = jnp.einsum('blgi,gpi->blgp', u, bb_re)
    drive_im = jnp.einsum('blgi,gpi->blgp', u, bb_im)
    seq_len = u.shape[1]
    a_shape = (1, seq_len) + ar.shape
    a_re = jnp.broadcast_to(ar, a_shape)
    a_im = jnp.broadcast_to(ai, a_shape)
    _, _, h_re, h_im = lax.associative_scan(
        _complex_affine_combine, (a_re, a_im, drive_re, drive_im), reverse=reverse, axis=1)
    return (jnp.einsum('blgp,gop->blgo', h_re, c_re.astype(f32))
            - jnp.einsum('blgp,gop->blgo', h_im, c_im.astype(f32)))


def _diff_attention(q, k, v, lam, lam_init, subln_g):
    f32 = jnp.float32
    bsz, seq_len = q.shape[0], q.shape[1]
    n_blocks = -(-seq_len // Q_BLOCK)
    l_pad = n_blocks * Q_BLOCK
    q_pad = jnp.pad(q, ((0, 0), (0, l_pad - seq_len), (0, 0), (0, 0), (0, 0)))
    q_blocks = q_pad.reshape(bsz, n_blocks, Q_BLOCK, ATTN_HEADS, 2, ATTN_HEAD_DIM)
    q_blocks = q_blocks.transpose(1, 0, 2, 3, 4, 5)
    starts = jnp.arange(n_blocks, dtype=jnp.int32) * Q_BLOCK
    key_pos = jnp.arange(seq_len, dtype=jnp.int32)
    slopes = jnp.exp2(-ALIBI_MAX_EXP * jnp.arange(1, ATTN_HEADS + 1, dtype=f32) / ATTN_HEADS)
    scale = ATTN_HEAD_DIM ** -0.5

    def block(args):
        qb, start = args
        qpos = start + jnp.arange(Q_BLOCK, dtype=jnp.int32)
        dist = jnp.abs(qpos[:, None] - key_pos[None, :]).astype(f32)
        is_meta = (qpos[:, None] < N_META) | (key_pos[None, :] < N_META)
        dist = jnp.where(is_meta, 0.0, dist)
        s = jnp.einsum('bqhcd,bkhcd->bhcqk', qb, k).astype(f32) * scale
        s = s - slopes[:, None, None, None] * dist
        p = jax.nn.softmax(s, axis=-1)
        a = p[:, :, 0] - lam * p[:, :, 1]
        return jnp.einsum('bhqk,bkhe->bqhe', a.astype(v.dtype), v)

    o = lax.map(block, (q_blocks, starts))
    o = o.transpose(1, 0, 2, 3, 4).reshape(bsz, l_pad, ATTN_HEADS, ATTN_V_DIM)[:, :seq_len]
    o = _rmsnorm(o, subln_g) * (1.0 - lam_init)
    return o.reshape(bsz, seq_len, ATTN_V_WIDTH)


def _moe(xf, w_router, b_router, w1, b1, w2, b2):
    f32 = jnp.float32
    n_tok = xf.shape[0]
    logits = (xf @ w_router).astype(f32) + b_router.astype(f32)
    top_vals, top_idx = lax.top_k(logits, TOP_K)
    gates = jax.nn.softmax(top_vals, axis=-1)
    n_assign = n_tok * TOP_K
    e_flat = top_idx.reshape(n_assign)
    tok_flat = jnp.broadcast_to(jnp.arange(n_tok, dtype=jnp.int32)[:, None], (n_tok, TOP_K)).reshape(n_assign)
    g_flat = gates.reshape(n_assign)
    order = jnp.argsort(e_flat)
    e_sorted = e_flat[order]
    tok_sorted = tok_flat[order]
    g_sorted = g_flat[order]
    counts = jnp.bincount(e_flat, length=N_EXPERTS)
    padded = ((counts + MOE_BLOCK - 1) // MOE_BLOCK) * MOE_BLOCK
    pad_end = jnp.cumsum(padded)
    pad_start = pad_end - padded
    cnt_start = jnp.cumsum(counts) - counts
    slot = pad_start[e_sorted] + (jnp.arange(n_assign, dtype=jnp.int32) - cnt_start[e_sorted])
    n_blocks = -(-n_assign // MOE_BLOCK) + N_EXPERTS
    n_slots = n_blocks * MOE_BLOCK
    slot_tok = jnp.zeros((n_slots,), jnp.int32).at[slot].set(tok_sorted)
    slot_gate = jnp.zeros((n_slots,), f32).at[slot].set(g_sorted)
    block_start = jnp.arange(n_blocks, dtype=jnp.int32) * MOE_BLOCK
    block_expert = jnp.minimum(jnp.searchsorted(pad_end, block_start, side='right'), N_EXPERTS - 1)

    def run_block(args):
        tok, gate_b, e = args
        xb = xf[tok]
        hdn = xb @ w1[e] + b1[e]
        x_glu = jnp.minimum(hdn[:, 0::2], SWIGLU_LIMIT)
        x_lin = jnp.clip(hdn[:, 1::2], -SWIGLU_LIMIT, SWIGLU_LIMIT)
        act = x_glu * jax.nn.sigmoid(SWIGLU_ALPHA * x_glu) * (x_lin + 1.0)
        y = act @ w2[e] + b2[e]
        return y * gate_b[:, None].astype(y.dtype)

    ys = lax.map(run_block, (slot_tok.reshape(n_blocks, MOE_BLOCK),
                             slot_gate.reshape(n_blocks, MOE_BLOCK), block_expert))
    return jnp.zeros_like(xf).at[slot_tok].add(ys.reshape(n_slots, xf.shape[1]).astype(xf.dtype))


def setup_inputs(seed: int = 0) -> dict:
    key = jax.random.key(seed)
    ks = jax.random.split(key, 40)
    f32 = jnp.float32

    def nrm(k, shape, scale):
        return scale * jax.random.normal(k, shape, f32)

    n_idx = jnp.arange(SSM_STATE, dtype=f32)

    def ssm_params(k0):
        kk = jax.random.split(k0, 7)
        lam_re = -0.5 + nrm(kk[0], (DEPTH, SSM_GROUPS, SSM_STATE), 0.01)
        lam_im = math.pi * n_idx + nrm(kk[1], (DEPTH, SSM_GROUPS, SSM_STATE), 0.01)
        log_dt = math.log(DT_MIN) + jax.random.uniform(kk[2], (DEPTH, SSM_GROUPS), f32) * (math.log(DT_MAX) - math.log(DT_MIN))
        b_re = nrm(kk[3], (DEPTH, SSM_GROUPS, SSM_STATE, SSM_GROUP), (2 * SSM_GROUP) ** -0.5)
        b_im = nrm(kk[4], (DEPTH, SSM_GROUPS, SSM_STATE, SSM_GROUP), (2 * SSM_GROUP) ** -0.5)
        c_re = nrm(kk[5], (DEPTH, SSM_GROUPS, SSM_GROUP, SSM_STATE), SSM_STATE ** -0.5)
        c_im = nrm(kk[6], (DEPTH, SSM_GROUPS, SSM_GROUP, SSM_STATE), SSM_STATE ** -0.5)
        return lam_re, lam_im, log_dt, b_re, b_im, c_re, c_im

    fw = ssm_params(ks[4])
    bw = ssm_params(ks[5])
    return {
        'x': nrm(ks[0], (BATCH, SEQ, D_MODEL), 1.0),
        'meta_tokens': nrm(ks[1], (N_META, D_MODEL), 1.0),
        'norm_mix_g': 1.0 + nrm(ks[2], (DEPTH, D_MODEL), 0.02),
        'w_in': nrm(ks[3], (DEPTH, D_MODEL, IN_COLS), D_MODEL ** -0.5),
        'ssm_lam_re_f': fw[0], 'ssm_lam_im_f': fw[1], 'ssm_log_dt_f': fw[2],
        'ssm_b_re_f': fw[3], 'ssm_b_im_f': fw[4], 'ssm_c_re_f': fw[5], 'ssm_c_im_f': fw[6],
        'ssm_lam_re_b': bw[0], 'ssm_lam_im_b': bw[1], 'ssm_log_dt_b': bw[2],
        'ssm_b_re_b': bw[3], 'ssm_b_im_b': bw[4], 'ssm_c_re_b': bw[5], 'ssm_c_im_b': bw[6],
        'ssm_d': nrm(ks[6], (DEPTH, SSM_WIDTH), 1.0),
        'w_ssm_glu': nrm(ks[7], (DEPTH, SSM_WIDTH, 2 * D_MODEL), SSM_WIDTH ** -0.5),
        'diff_lam_q1': nrm(ks[8], (DEPTH, ATTN_HEAD_DIM), 0.1),
        'diff_lam_k1': nrm(ks[9], (DEPTH, ATTN_HEAD_DIM), 0.1),
        'diff_lam_q2': nrm(ks[10], (DEPTH, ATTN_HEAD_DIM), 0.1),
        'diff_lam_k2': nrm(ks[11], (DEPTH, ATTN_HEAD_DIM), 0.1),
        'diff_subln_g': 1.0 + nrm(ks[12], (DEPTH, ATTN_V_DIM), 0.02),
        'w_attn_out': nrm(ks[13], (DEPTH, ATTN_V_WIDTH, D_MODEL), ATTN_V_WIDTH ** -0.5),
        'w_out': nrm(ks[14], (DEPTH, D_MODEL, D_MODEL), D_MODEL ** -0.5),
        'norm_moe_g': 1.0 + nrm(ks[15], (DEPTH, D_MODEL), 0.02),
        'w_router': nrm(ks[16], (DEPTH, D_MODEL, N_EXPERTS), D_MODEL ** -0.5),
        'b_router': nrm(ks[17], (DEPTH, N_EXPERTS), 0.01),
        'w_exp1': nrm(ks[18], (DEPTH, N_EXPERTS, D_MODEL, 2 * D_FF), D_MODEL ** -0.5),
        'b_exp1': nrm(ks[19], (DEPTH, N_EXPERTS, 2 * D_FF), 0.01),
        'w_exp2': nrm(ks[20], (DEPTH, N_EXPERTS, D_FF, D_MODEL), D_FF ** -0.5),
        'b_exp2': nrm(ks[21], (DEPTH, N_EXPERTS, D_MODEL), 0.01),
        'norm_final_g': 1.0 + nrm(ks[22], (D_MODEL,), 0.02),
    }


def reference(x, meta_tokens, norm_mix_g, w_in,
              ssm_lam_re_f, ssm_lam_im_f, ssm_log_dt_f, ssm_b_re_f, ssm_b_im_f, ssm_c_re_f, ssm_c_im_f,
              ssm_lam_re_b, ssm_lam_im_b, ssm_log_dt_b, ssm_b_re_b, ssm_b_im_b, ssm_c_re_b, ssm_c_im_b,
              ssm_d, w_ssm_glu,
              diff_lam_q1, diff_lam_k1, diff_lam_q2, diff_lam_k2, diff_subln_g, w_attn_out,
              w_out, norm_moe_g, w_router, b_router, w_exp1, b_exp1, w_exp2, b_exp2, norm_final_g):
    f32 = jnp.float32
    bsz = x.shape[0]
    meta = jnp.broadcast_to(meta_tokens.astype(x.dtype)[None], (bsz, N_META, D_MODEL))
    h = jnp.concatenate([meta, x], axis=1)
    seq_len = h.shape[1]
    c0 = SSM_WIDTH
    c1 = c0 + ATTN_QK_WIDTH
    c2 = c1 + ATTN_QK_WIDTH
    c3 = c2 + ATTN_V_WIDTH
    c4 = c3 + D_MODEL
    for layer in range(DEPTH):
        xn = _rmsnorm(h, norm_mix_g[layer])
        proj = xn @ w_in[layer]
        u, q, k, v, gate_s, gate_a = jnp.split(proj, [c0, c1, c2, c3, c4], axis=-1)

        uf = u.astype(f32)
        ug = uf.reshape(bsz, seq_len, SSM_GROUPS, SSM_GROUP)
        y = (_s5_direction(ug, ssm_lam_re_f[layer], ssm_lam_im_f[layer], ssm_log_dt_f[layer],
                           ssm_b_re_f[layer], ssm_b_im_f[layer], ssm_c_re_f[layer], ssm_c_im_f[layer], False)
             + _s5_direction(ug, ssm_lam_re_b[layer], ssm_lam_im_b[layer], ssm_log_dt_b[layer],
                             ssm_b_re_b[layer], ssm_b_im_b[layer], ssm_c_re_b[layer], ssm_c_im_b[layer], True))
        y = y.reshape(bsz, seq_len, SSM_WIDTH) + ssm_d[layer].astype(f32) * uf
        y = jax.nn.gelu(y).astype(h.dtype)
        glu_a, glu_b = jnp.split(y @ w_ssm_glu[layer], 2, axis=-1)
        y_ssm = glu_a * jax.nn.sigmoid(glu_b)

        lam_init = 0.8 - 0.6 * math.exp(-0.3 * layer)
        lam = (jnp.exp(jnp.sum(diff_lam_q1[layer].astype(f32) * diff_lam_k1[layer].astype(f32)))
               - jnp.exp(jnp.sum(diff_lam_q2[layer].astype(f32) * diff_lam_k2[layer].astype(f32)))
               + lam_init)
        o = _diff_attention(q.reshape(bsz, seq_len, ATTN_HEADS, 2, ATTN_HEAD_DIM),
                            k.reshape(bsz, seq_len, ATTN_HEADS, 2, ATTN_HEAD_DIM),
                            v.reshape(bsz, seq_len, ATTN_HEADS, ATTN_V_DIM),
                            lam, lam_init, diff_subln_g[layer])
        y_attn = o @ w_attn_out[layer]

        merged = jax.nn.sigmoid(gate_s) * y_ssm + jax.nn.sigmoid(gate_a) * y_attn
        h = h + merged @ w_out[layer]

        hn = _rmsnorm(h, norm_moe_g[layer])
        moe_out = _moe(hn.reshape(bsz * seq_len, D_MODEL), w_router[layer], b_router[layer],
                       w_exp1[layer], b_exp1[layer], w_exp2[layer], b_exp2[layer])
        h = h + moe_out.reshape(bsz, seq_len, D_MODEL)
    h = _rmsnorm(h, norm_final_g)
    return h[:, N_META:]
```

```python
import functools
import math

import jax
import jax.numpy as jnp
from jax import lax
from jax.experimental import pallas as pl
from jax.experimental.pallas import tpu as pltpu

F32 = jnp.float32
BF16 = jnp.bfloat16

N_META = 16
NORM_EPS = 1e-5
SSM_GROUP = 16
SSM_STATE = 64
ATTN_HEADS = 8
ATTN_HEAD_DIM = 64
ATTN_V_DIM = 2 * ATTN_HEAD_DIM
ALIBI_MAX_EXP = 8.0
TOP_K = 4
SWIGLU_LIMIT = 7.0
SWIGLU_ALPHA = 1.702

LANES = 128
SSM_CHUNK = 32
MOE_BLOCK = 512
VMEM_LIMIT = 56 * 1024 * 1024
NEG_BIG = -1e30
HIGHEST = lax.Precision.HIGHEST


def _div_tile(n, target, mult):
    best = None
    for t in range(mult, min(n, target) + 1, mult):
        if n % t == 0:
            best = t
    assert best is not None, (n, target, mult)
    return best


def _params(n_axes):
    return pltpu.CompilerParams(dimension_semantics=("arbitrary",) * n_axes,
                                vmem_limit_bytes=VMEM_LIMIT)


def _rms(x, g):
    return x * lax.rsqrt(jnp.mean(x * x, axis=-1, keepdims=True) + NORM_EPS) * g


def _inproj_kernel(h_ref, g_ref, w_ref, o_ref, xn_ref):
    @pl.when(pl.program_id(1) == 0)
    def _():
        xn_ref[...] = _rms(h_ref[...], g_ref[...]).astype(BF16)

    o_ref[...] = jnp.dot(xn_ref[...], w_ref[...], preferred_element_type=F32).astype(o_ref.dtype)


def _inproj(h, g, w_bf16):
    tp, d = h.shape
    n = w_bf16.shape[1]
    tm = _div_tile(tp, 640, LANES)
    tn = _div_tile(n, 1024, LANES)
    return pl.pallas_call(
        _inproj_kernel,
        grid=(tp // tm, n // tn),
        in_specs=[pl.BlockSpec((tm, d), lambda i, j: (i, 0)),
                  pl.BlockSpec((1, d), lambda i, j: (0, 0)),
                  pl.BlockSpec((d, tn), lambda i, j: (0, j))],
        out_specs=pl.BlockSpec((tm, tn), lambda i, j: (i, j)),
        out_shape=jax.ShapeDtypeStruct((tp, n), BF16),
        scratch_shapes=[pltpu.VMEM((tm, d), BF16)],
        compiler_params=_params(2),
        name="inproj",
    )(h, g, w_bf16)


def _ssm_direction_terms(lam_re, lam_im, log_dt, b_re, b_im):
    dt = jnp.exp(log_dt)[:, None]
    zr = lam_re * dt
    zi = lam_im * dt

    def apow(n):
        n = n.astype(F32)[:, None, None]
        mag = jnp.exp(n * zr[None])
        return mag * jnp.cos(n * zi[None]), mag * jnp.sin(n * zi[None])

    ar, ai = apow(jnp.ones((1,), F32))
    ar, ai = ar[0], ai[0]
    den = lam_re * lam_re + lam_im * lam_im
    fr = ((ar - 1.0) * lam_re + ai * lam_im) / den
    fi = (ai * lam_re - (ar - 1.0) * lam_im) / den
    bbr = fr[..., None] * b_re - fi[..., None] * b_im
    bbi = fr[..., None] * b_im + fi[..., None] * b_re
    return apow, bbr, bbi


def _pad_last(x, n):
    return jnp.pad(x, [(0, 0)] * (x.ndim - 1) + [(0, n - x.shape[-1])])


def _ssm_operators(fw, bw, ssm_d, tc, ncs):
    g_n, p_n = fw[0].shape
    io = SSM_GROUP
    n = jnp.arange(tc)
    terms = []
    for (lam_re, lam_im, log_dt, b_re, b_im, c_re, c_im), rev in ((fw, False), (bw, True)):
        apow, bbr, bbi = _ssm_direction_terms(lam_re, lam_im, log_dt, b_re, b_im)
        er, ei = apow(n)
        mr = c_re[None] * er[:, :, None, :] - c_im[None] * ei[:, :, None, :]
        mi = c_re[None] * ei[:, :, None, :] + c_im[None] * er[:, :, None, :]
        kern = (jnp.einsum('ngop,gpi->ngoi', mr, bbr, precision=HIGHEST)
                - jnp.einsum('ngop,gpi->ngoi', mi, bbi, precision=HIGHEST))
        es_r, es_i = (er, ei) if rev else (er[::-1], ei[::-1])
        st_re = es_r[..., None] * bbr[None] - es_i[..., None] * bbi[None]
        st_im = es_r[..., None] * bbi[None] + es_i[..., None] * bbr[None]
        st = [_pad_last(x.transpose(1, 0, 3, 2).reshape(g_n, tc * io, p_n), LANES)
              for x in (st_re, st_im)]
        cr_, ci_ = apow((tc - n) if rev else (n + 1))
        rr = c_re[None] * cr_[:, :, None, :] - c_im[None] * ci_[:, :, None, :]
        ri = c_re[None] * ci_[:, :, None, :] + c_im[None] * cr_[:, :, None, :]
        ro = [jnp.pad(x.transpose(1, 3, 0, 2).reshape(g_n, p_n, tc * io),
                      ((0, 0), (0, LANES - p_n), (0, 0))) for x in (rr, -ri)]
        dr, di = apow(jnp.array([tc, tc * ncs]))
        dec = [_pad_last(dr, LANES), _pad_last(di, LANES)]
        terms.append((kern, st, ro, dec))
    (kf, stf, rof, decf), (kb, stb, rob, decb) = terms
    dmat = jnp.eye(io, dtype=F32)[None] * ssm_d.reshape(g_n, io)[:, :, None]
    kcat = jnp.concatenate([kb[:0:-1], kf[:1] + kb[:1] + dmat[None], kf[1:]], axis=0)
    idx = (n[None, :] - n[:, None]) + (tc - 1)
    toep = kcat[idx]
    toep = toep.transpose(2, 0, 4, 1, 3).reshape(g_n, tc * io, tc * io)
    w1 = jnp.concatenate([toep] + stf + stb, axis=-1).astype(BF16)
    cmat = jnp.concatenate(rof + rob, axis=1).astype(BF16)
    chunk_seg = [jnp.stack([decf[0][k], decf[1][k], decb[0][k], decb[1][k]], axis=1) for k in (0, 1)]
    decay = jnp.concatenate(chunk_seg, axis=1)
    return w1, cmat, decay


def _gelu_tanh(y):
    return 0.5 * y * (1.0 + jnp.tanh(math.sqrt(2.0 / math.pi) * (y + 0.044715 * (y * y * y))))


def _cmul(ar, ai, xr, xi):
    return ar * xr - ai * xi, ar * xi + ai * xr


def _ssm_kernel(u_ref, w_ref, c_ref, dec_ref, o_ref, s_ref, h_ref, *, ncs, nb, n_toep):
    z = jnp.dot(u_ref[0], w_ref[0], preferred_element_type=F32)
    s_ref[...] = z[:, n_toep:]
    dec = dec_ref[0]
    afr, afi, abr, abi = dec[0:1], dec[1:2], dec[2:3], dec[3:4]
    sfr, sfi, sbr, sbi = dec[4:5], dec[5:6], dec[6:7], dec[7:8]
    lf_re, lf_im = slice(0, LANES), slice(LANES, 2 * LANES)
    lb_re, lb_im = slice(2 * LANES, 3 * LANES), slice(3 * LANES, 4 * LANES)

    def tiles(c):
        return (pl.ds(pl.multiple_of(c * 8, 8), 8), pl.ds(pl.multiple_of((ncs - 1 - c) * 8, 8), 8))

    def local(c, carry):
        hfr, hfi, hbr, hbi = carry
        rf, rb = tiles(c)
        h_ref[rf, lf_re] = hfr
        h_ref[rf, lf_im] = hfi
        h_ref[rb, lb_re] = hbr
        h_ref[rb, lb_im] = hbi
        nfr, nfi = _cmul(afr, afi, hfr, hfi)
        nbr, nbi = _cmul(abr, abi, hbr, hbi)
        return (nfr + s_ref[rf, lf_re], nfi + s_ref[rf, lf_im],
                nbr + s_ref[rb, lb_re], nbi + s_ref[rb, lb_im])

    zero = jnp.zeros((8, LANES), F32)
    efr, efi, ebr, ebi = lax.fori_loop(0, ncs, local, (zero, zero, zero, zero))

    row8 = lax.broadcasted_iota(jnp.int32, (8, LANES), 0)

    def shift_dn(x):
        return jnp.where(row8 >= nb, pltpu.roll(x, nb, 0), 0.0)

    def shift_up(x):
        return jnp.where(row8 < 8 - nb, pltpu.roll(x, 8 - nb, 0), 0.0)

    def segment_in(er, ei, ar, ai, shift):
        xr, xi = shift(er), shift(ei)
        ir, ii = xr, xi
        for _ in range(8 // nb - 2):
            tr, ti = _cmul(ar, ai, shift(ir), shift(ii))
            ir, ii = xr + tr, xi + ti
        return ir, ii

    ifr, ifi = segment_in(efr, efi, sfr, sfi, shift_dn)
    ibr, ibi = segment_in(ebr, ebi, sbr, sbi, shift_up)

    def fix(c, carry):
        pfr, pfi, pbr, pbi = carry
        rf, rb = tiles(c)
        dfr, dfi = _cmul(pfr, pfi, ifr, ifi)
        dbr, dbi = _cmul(pbr, pbi, ibr, ibi)
        h_ref[rf, lf_re] = h_ref[rf, lf_re] + dfr
        h_ref[rf, lf_im] = h_ref[rf, lf_im] + dfi
        h_ref[rb, lb_re] = h_ref[rb, lb_re] + dbr
        h_ref[rb, lb_im] = h_ref[rb, lb_im] + dbi
        return _cmul(afr, afi, pfr, pfi) + _cmul(abr, abi, pbr, pbi)

    one = jnp.ones((8, LANES), F32)
    lax.fori_loop(0, ncs, fix, (one, zero, one, zero))

    y = z[:, :n_toep] + jnp.dot(h_ref[...].astype(BF16), c_ref[0], preferred_element_type=F32)
    o_ref[0] = _gelu_tanh(y).astype(o_ref.dtype)


def _ssm(u_g, w1, cmat, decay, ncs, nb):
    g_n, rows, n_toep = u_g.shape
    n_w = w1.shape[-1]
    n_st = n_w - n_toep
    return pl.pallas_call(
        functools.partial(_ssm_kernel, ncs=ncs, nb=nb, n_toep=n_toep),
        grid=(g_n,),
        in_specs=[pl.BlockSpec((1, rows, n_toep), lambda g: (g, 0, 0)),
                  pl.BlockSpec((1, n_toep, n_w), lambda g: (g, 0, 0)),
                  pl.BlockSpec((1, n_st, n_toep), lambda g: (g, 0, 0)),
                  pl.BlockSpec((1, 8, LANES), lambda g: (g, 0, 0))],
        out_specs=pl.BlockSpec((1, rows, n_toep), lambda g: (g, 0, 0)),
        out_shape=jax.ShapeDtypeStruct((g_n, rows, n_toep), BF16),
        scratch_shapes=[pltpu.VMEM((rows, n_st), F32), pltpu.VMEM((rows, n_st), F32)],
        compiler_params=_params(1),
        name="ssm",
    )(u_g, w1, cmat, decay)


def _attn_kernel(slope_ref, lam_ref, q_ref, k_ref, v_ref, g_ref, o_ref, *, tq, tk, seq, lam_init):
    slope = slope_ref[pl.program_id(1)]
    lam = lam_ref[0]
    q0 = pl.program_id(2) * tq
    q = q_ref[...] * jnp.asarray(ATTN_HEAD_DIM ** -0.5, BF16)
    lane = lax.broadcasted_iota(jnp.int32, q.shape, 1)
    zero = jnp.zeros_like(q)
    q1 = jnp.where(lane < ATTN_HEAD_DIM, q, zero)
    q2 = jnp.where(lane >= ATTN_HEAD_DIM, q, zero)
    qpos = q0 + lax.broadcasted_iota(jnp.int32, (tq, 1), 0)
    nt = (((1,), (1,)), ((), ()))

    def online(s, v, m, l, acc):
        m_new = jnp.maximum(m, jnp.max(s, axis=-1, keepdims=True))
        alpha = jnp.exp(m - m_new)
        p = jnp.exp(s - m_new)
        l = alpha * l + jnp.sum(p, axis=-1, keepdims=True)
        acc = alpha * acc + jnp.dot(p.astype(BF16), v, preferred_element_type=F32)
        return m_new, l, acc

    def step(j, carry):
        m1, l1, a1, m2, l2, a2 = carry
        k = k_ref[pl.ds(j * tk, tk), :]
        v = v_ref[pl.ds(j * tk, tk), :]
        kpos = j * tk + lax.broadcasted_iota(jnp.int32, (1, tk), 1)
        dist = jnp.abs(qpos - kpos).astype(F32)
        dist = jnp.where((qpos < N_META) | (kpos < N_META), 0.0, dist)
        bias = slope * dist
        live = kpos < seq
        s1 = lax.dot_general(q1, k, nt, preferred_element_type=F32)
        s2 = lax.dot_general(q2, k, nt, preferred_element_type=F32)
        s1 = jnp.where(live, s1 - bias, NEG_BIG)
        s2 = jnp.where(live, s2 - bias, NEG_BIG)
        m1, l1, a1 = online(s1, v, m1, l1, a1)
        m2, l2, a2 = online(s2, v, m2, l2, a2)
        return m1, l1, a1, m2, l2, a2

    m0 = jnp.full((tq, 1), NEG_BIG, F32)
    l0 = jnp.zeros((tq, 1), F32)
    a0 = jnp.zeros((tq, ATTN_V_DIM), F32)
    _, l1, a1, _, l2, a2 = lax.fori_loop(0, k_ref.shape[0] // tk, step, (m0, l0, a0, m0, l0, a0))
    o = a1 / l1 - lam * (a2 / l2)
    o_ref[...] = (_rms(o, g_ref[...]) * (1.0 - lam_init)).astype(o_ref.dtype)


def _attention(proj, slopes, lam, subln_g, bsz, lp, seq, col0, lam_init):
    tp = proj.shape[0]
    tq = _div_tile(lp, 320, 8)
    tk = _div_tile(lp, 640, LANES)
    nq = lp // tq
    qc, kc, vc = (col0 // LANES + i * ATTN_HEADS for i in range(3))
    return pl.pallas_call(
        functools.partial(_attn_kernel, tq=tq, tk=tk, seq=seq, lam_init=lam_init),
        grid=(bsz, ATTN_HEADS, nq),
        in_specs=[pl.BlockSpec(memory_space=pltpu.SMEM),
                  pl.BlockSpec(memory_space=pltpu.SMEM),
                  pl.BlockSpec((tq, LANES), lambda b, h, i: (b * nq + i, qc + h)),
                  pl.BlockSpec((lp, LANES), lambda b, h, i: (b, kc + h)),
                  pl.BlockSpec((lp, LANES), lambda b, h, i: (b, vc + h)),
                  pl.BlockSpec((1, LANES), lambda b, h, i: (0, 0))],
        out_specs=pl.BlockSpec((tq, LANES), lambda b, h, i: (b * nq + i, h)),
        out_shape=jax.ShapeDtypeStruct((tp, ATTN_HEADS * ATTN_V_DIM), BF16),
        compiler_params=_params(3),
        name="attn",
    )(slopes, lam, proj, proj, proj, subln_g)


def _merge_kernel(y_ref, o_ref, gs_ref, ga_ref, wa_ref, wb_ref, wo_ref, out_ref):
    y = y_ref[...]
    glu_a = jnp.dot(y, wa_ref[...], preferred_element_type=F32)
    glu_b = jnp.dot(y, wb_ref[...], preferred_element_type=F32)
    y_attn = jnp.dot(o_ref[...], wo_ref[...], preferred_element_type=F32)
    y_ssm = glu_a * jax.nn.sigmoid(glu_b)
    merged = (jax.nn.sigmoid(gs_ref[...].astype(F32)) * y_ssm
              + jax.nn.sigmoid(ga_ref[...].astype(F32)) * y_attn)
    out_ref[...] = merged.astype(out_ref.dtype)


def _merge(y_gelu, o_attn, proj, w_glu, w_ao, gate_col0, d):
    tp, w = y_gelu.shape
    tm = _div_tile(tp, 640, LANES)
    tn = _div_tile(d, 512, LANES)
    nj = d // tn
    gs0 = gate_col0 // tn
    return pl.pallas_call(
        _merge_kernel,
        grid=(tp // tm, nj),
        in_specs=[pl.BlockSpec((tm, w), lambda i, j: (i, 0)),
                  pl.BlockSpec((tm, o_attn.shape[1]), lambda i, j: (i, 0)),
                  pl.BlockSpec((tm, tn), lambda i, j: (i, gs0 + j)),
                  pl.BlockSpec((tm, tn), lambda i, j: (i, gs0 + nj + j)),
                  pl.BlockSpec((w, tn), lambda i, j: (0, j)),
                  pl.BlockSpec((w, tn), lambda i, j: (0, nj + j)),
                  pl.BlockSpec((w_ao.shape[0], tn), lambda i, j: (0, j))],
        out_specs=pl.BlockSpec((tm, tn), lambda i, j: (i, j)),
        out_shape=jax.ShapeDtypeStruct((tp, d), BF16),
        compiler_params=_params(2),
        name="merge",
    )(y_gelu, o_attn, proj, proj, w_glu, w_glu, w_ao)


def _outproj_router_kernel(m_ref, h_ref, wo_ref, g_ref, wr_ref, br_ref,
                           h1_ref, hn_ref, idx_ref, gate_ref, cnt_ref, *, tm, lp, seq, bsz):
    i = pl.program_id(0)
    n_e = wr_ref.shape[1]

    @pl.when(i == 0)
    def _():
        cnt_ref[...] = jnp.zeros_like(cnt_ref)

    h1 = h_ref[...] + jnp.dot(m_ref[...], wo_ref[...], preferred_element_type=F32)
    h1_ref[...] = h1
    hn = _rms(h1, g_ref[...])
    hn_ref[...] = hn
    logits = jnp.dot(hn, wr_ref[...], preferred_element_type=F32, precision=HIGHEST) + br_ref[...]

    row = i * tm + lax.broadcasted_iota(jnp.int32, (tm, 1), 0)
    valid = row < 0
    for b in range(bsz):
        valid = valid | ((row >= b * lp) & (row < b * lp + seq))

    lane = lax.broadcasted_iota(jnp.int32, (tm, n_e), 1).astype(F32)
    work = logits
    vals, idxs, hots = [], [], []
    for _ in range(TOP_K):
        mk = jnp.max(work, axis=-1, keepdims=True)
        ik = jnp.min(jnp.where(work == mk, lane, float(n_e)), axis=-1, keepdims=True)
        hot = lane == ik
        work = jnp.where(hot, -jnp.inf, work)
        vals.append(mk)
        idxs.append(ik)
        hots.append(hot)
    exps = [jnp.exp(v - vals[0]) for v in vals]
    denom = exps[0]
    for e in exps[1:]:
        denom = denom + e

    multi = jnp.zeros((tm, n_e), F32)
    for hot in hots:
        multi = multi + jnp.where(hot & valid, 1.0, 0.0)
    r_i = lax.broadcasted_iota(jnp.int32, (tm, tm), 0)
    c_i = lax.broadcasted_iota(jnp.int32, (tm, tm), 1)
    tri = jnp.where(r_i > c_i, 1.0, 0.0).astype(BF16)
    before = jnp.dot(tri, multi.astype(BF16), preferred_element_type=F32) + cnt_ref[0:1, 0:n_e]

    lane_o = lax.broadcasted_iota(jnp.int32, (tm, LANES), 1)
    idx_out = jnp.zeros((tm, LANES), jnp.int32)
    gate_out = jnp.zeros((tm, LANES), F32)
    for k in range(TOP_K):
        rank = jnp.sum(jnp.where(hots[k], before, 0.0), axis=-1, keepdims=True).astype(jnp.int32)
        idx_out = jnp.where(lane_o == k, idxs[k].astype(jnp.int32), idx_out)
        idx_out = jnp.where(lane_o == TOP_K + k, rank, idx_out)
        gate_out = jnp.where(lane_o == k, exps[k] / denom, gate_out)
    idx_ref[...] = idx_out
    gate_ref[...] = gate_out
    cnt_ref[0:1, 0:n_e] = cnt_ref[0:1, 0:n_e] + jnp.sum(multi, axis=0, keepdims=True)


def _outproj_router(merged, h, w_out, g, w_router, b_router, lp, seq, bsz):
    tp, d = h.shape
    n_e = w_router.shape[1]
    tm = _div_tile(tp, 256, LANES)
    row = lambda i: (i, 0)
    fixed = lambda i: (0, 0)
    return pl.pallas_call(
        functools.partial(_outproj_router_kernel, tm=tm, lp=lp, seq=seq, bsz=bsz),
        grid=(tp // tm,),
        in_specs=[pl.BlockSpec((tm, d), row), pl.BlockSpec((tm, d), row),
                  pl.BlockSpec((d, d), fixed), pl.BlockSpec((1, d), fixed),
                  pl.BlockSpec((d, n_e), fixed), pl.BlockSpec((1, n_e), fixed)],
        out_specs=[pl.BlockSpec((tm, d), row), pl.BlockSpec((tm, d), row),
                   pl.BlockSpec((tm, LANES), row), pl.BlockSpec((tm, LANES), row),
                   pl.BlockSpec((8, LANES), fixed)],
        out_shape=[jax.ShapeDtypeStruct((tp, d), F32), jax.ShapeDtypeStruct((tp, d), F32),
                   jax.ShapeDtypeStruct((tp, LANES), jnp.int32), jax.ShapeDtypeStruct((tp, LANES), F32),
                   jax.ShapeDtypeStruct((8, LANES), F32)],
        compiler_params=_params(1),
        name="outproj_router",
    )(merged, h, w_out, g, w_router, b_router)


def _row_copy(src_hbm, row, dst, dst_row, sem):
    return pltpu.make_async_copy(src_hbm.at[pl.ds(row, 1), :], dst.at[pl.ds(dst_row, 1), :], sem)


def _dispatch_kernel(tok_ref, nused_ref, hn_hbm, o_ref, sem, *, pb):
    i = pl.program_id(0)

    @pl.when(i < nused_ref[0])
    def _():
        def issue(r, c):
            _row_copy(hn_hbm, tok_ref[i * pb + r], o_ref, r, sem).start()
            return c

        lax.fori_loop(0, pb, issue, 0)

        def drain(r, c):
            _row_copy(hn_hbm, 0, o_ref, r, sem).wait()
            return c

        lax.fori_loop(0, pb, drain, 0)

    @pl.when(i >= nused_ref[0])
    def _():
        o_ref[...] = jnp.zeros_like(o_ref)


def _dispatch(slot_tok, n_used, hn, n_blocks, pb):
    d = hn.shape[1]
    return pl.pallas_call(
        functools.partial(_dispatch_kernel, pb=pb),
        grid_spec=pltpu.PrefetchScalarGridSpec(
            num_scalar_prefetch=2,
            grid=(n_blocks,),
            in_specs=[pl.BlockSpec(memory_space=pl.ANY)],
            out_specs=pl.BlockSpec((pb, d), lambda i, tok, nu: (i, 0)),
            scratch_shapes=[pltpu.SemaphoreType.DMA]),
        out_shape=jax.ShapeDtypeStruct((n_blocks * pb, d), hn.dtype),
        compiler_params=_params(1),
        name="dispatch",
    )(slot_tok, n_used, hn)


def _ffn1_kernel(be_ref, bm_ref, nused_ref, x_ref, w_ref, b_ref, o_ref):
    @pl.when(pl.program_id(1) < nused_ref[0])
    def _():
        x = x_ref[...].astype(BF16)
        bias = b_ref[0]
        hg = jnp.dot(x, w_ref[0, 0], preferred_element_type=F32) + bias[0:1]
        hl = jnp.dot(x, w_ref[0, 1], preferred_element_type=F32) + bias[1:2]
        glu = jnp.minimum(hg, SWIGLU_LIMIT)
        lin = jnp.clip(hl, -SWIGLU_LIMIT, SWIGLU_LIMIT)
        o_ref[...] = (glu * jax.nn.sigmoid(SWIGLU_ALPHA * glu) * (lin + 1.0)).astype(o_ref.dtype)

    @pl.when(pl.program_id(1) >= nused_ref[0])
    def _():
        o_ref[...] = jnp.zeros_like(o_ref)


def _ffn1(blk_expert, blk_map, n_used, xs, w1p, b1p, pb):
    n_slots, d = xs.shape
    f = w1p.shape[-1]
    tn = _div_tile(f, 1024, LANES)
    return pl.pallas_call(
        _ffn1_kernel,
        grid_spec=pltpu.PrefetchScalarGridSpec(
            num_scalar_prefetch=3,
            grid=(f // tn, n_slots // pb),
            in_specs=[pl.BlockSpec((pb, d), lambda j, i, be, bm, nu: (bm[i], 0)),
                      pl.BlockSpec((1, 2, d, tn), lambda j, i, be, bm, nu: (be[i], 0, 0, j)),
                      pl.BlockSpec((1, 2, tn), lambda j, i, be, bm, nu: (be[i], 0, j))],
            out_specs=pl.BlockSpec((pb, tn), lambda j, i, be, bm, nu: (i, j))),
        out_shape=jax.ShapeDtypeStruct((n_slots, f), BF16),
        compiler_params=_params(2),
        name="ffn1",
    )(blk_expert, blk_map, n_used, xs, w1p, b1p)


def _ffn2_kernel(be_ref, bm_ref, nused_ref, a_ref, w_ref, b_ref, o_ref):
    @pl.when(pl.program_id(1) < nused_ref[0])
    def _():
        o_ref[...] = jnp.dot(a_ref[...], w_ref[0], preferred_element_type=F32) + b_ref[0]

    @pl.when(pl.program_id(1) >= nused_ref[0])
    def _():
        o_ref[...] = jnp.zeros_like(o_ref)


def _ffn2(blk_expert, blk_map, n_used, act, w2, b2, pb):
    n_slots, f = act.shape
    d = w2.shape[-1]
    tn = _div_tile(d, 1024, LANES)
    return pl.pallas_call(
        _ffn2_kernel,
        grid_spec=pltpu.PrefetchScalarGridSpec(
            num_scalar_prefetch=3,
            grid=(d // tn, n_slots // pb),
            in_specs=[pl.BlockSpec((pb, f), lambda j, i, be, bm, nu: (bm[i], 0)),
                      pl.BlockSpec((1, f, tn), lambda j, i, be, bm, nu: (be[i], 0, j)),
                      pl.BlockSpec((1, 1, tn), lambda j, i, be, bm, nu: (be[i], 0, j))],
            out_specs=pl.BlockSpec((pb, tn), lambda j, i, be, bm, nu: (i, j))),
        out_shape=jax.ShapeDtypeStruct((n_slots, d), F32),
        compiler_params=_params(2),
        name="ffn2",
    )(blk_expert, blk_map, n_used, act, w2, b2)


def _combine_kernel(slot_ref, ys_hbm, h1_ref, gate_ref, g_ref, o_ref, buf, sem, *, tm):
    i = pl.program_id(0)

    def issue(r, c):
        for k in range(TOP_K):
            _row_copy(ys_hbm, slot_ref[(i * tm + r) * TOP_K + k], buf.at[k], r, sem).start()
        return c

    lax.fori_loop(0, tm, issue, 0)

    def drain(r, c):
        for k in range(TOP_K):
            _row_copy(ys_hbm, 0, buf.at[k], r, sem).wait()
        return c

    lax.fori_loop(0, tm, drain, 0)
    gates = gate_ref[...]
    h2 = h1_ref[...]
    for k in range(TOP_K):
        h2 = h2 + gates[:, k:k + 1] * buf[k]
    o_ref[...] = _rms(h2, g_ref[...])


def _combine(slot_flat, ys, h1, gates, g):
    tp, d = h1.shape
    tm = _div_tile(tp, 128, LANES)
    row = lambda i, s: (i, 0)
    return pl.pallas_call(
        functools.partial(_combine_kernel, tm=tm),
        grid_spec=pltpu.PrefetchScalarGridSpec(
            num_scalar_prefetch=1,
            grid=(tp // tm,),
            in_specs=[pl.BlockSpec(memory_space=pl.ANY),
                      pl.BlockSpec((tm, d), row), pl.BlockSpec((tm, LANES), row),
                      pl.BlockSpec((1, d), lambda i, s: (0, 0))],
            out_specs=pl.BlockSpec((tm, d), row),
            scratch_shapes=[pltpu.VMEM((TOP_K, tm, d), F32), pltpu.SemaphoreType.DMA]),
        out_shape=jax.ShapeDtypeStruct((tp, d), F32),
        compiler_params=_params(1),
        name="combine",
    )(slot_flat, ys, h1, gates, g)


def kernel(x, meta_tokens, norm_mix_g, w_in, ssm_lam_re_f, ssm_lam_im_f, ssm_log_dt_f, ssm_b_re_f, ssm_b_im_f, ssm_c_re_f, ssm_c_im_f, ssm_lam_re_b, ssm_lam_im_b, ssm_log_dt_b, ssm_b_re_b, ssm_b_im_b, ssm_c_re_b, ssm_c_im_b, ssm_d, w_ssm_glu, diff_lam_q1, diff_lam_k1, diff_lam_q2, diff_lam_k2, diff_subln_g, w_attn_out, w_out, norm_moe_g, w_router, b_router, w_exp1, b_exp1, w_exp2, b_exp2, norm_final_g):
    bsz, s_len, d = x.shape
    depth = w_in.shape[0]
    seq = s_len + N_META
    lp = -(-seq // LANES) * LANES
    tp = bsz * lp
    ssm_w = ssm_d.shape[-1]
    g_n = ssm_w // SSM_GROUP
    qk_w = ATTN_HEADS * 2 * ATTN_HEAD_DIM
    v_w = ATTN_HEADS * ATTN_V_DIM
    n_e = w_router.shape[-1]
    f_ff = w_exp2.shape[2]
    tc = SSM_CHUNK
    nc = lp // tc
    assert 8 % bsz == 0, "the S5 scan packs batch x sequence segments into 8-row tiles"
    nseg = 8 // bsz
    assert nc % nseg == 0
    ncs = nc // nseg
    pb = MOE_BLOCK

    meta = jnp.broadcast_to(meta_tokens.astype(x.dtype)[None], (bsz, N_META, d))
    h = jnp.concatenate([meta, x, jnp.zeros((bsz, lp - seq, d), x.dtype)], axis=1).reshape(tp, d)

    pos = jnp.arange(tp, dtype=jnp.int32) % lp
    tok_valid = pos < seq
    slopes = jnp.exp2(-ALIBI_MAX_EXP * jnp.arange(1, ATTN_HEADS + 1, dtype=F32) / ATTN_HEADS)

    for layer in range(depth):
        proj = _inproj(h, norm_mix_g[layer][None], w_in[layer].astype(BF16))

        u_g = proj[:, :ssm_w].reshape(bsz, nseg, ncs, tc, g_n, SSM_GROUP)
        u_g = u_g.transpose(4, 2, 1, 0, 3, 5).reshape(g_n, nc * bsz, tc * SSM_GROUP)
        fw = (ssm_lam_re_f, ssm_lam_im_f, ssm_log_dt_f, ssm_b_re_f, ssm_b_im_f, ssm_c_re_f, ssm_c_im_f)
        bw = (ssm_lam_re_b, ssm_lam_im_b, ssm_log_dt_b, ssm_b_re_b, ssm_b_im_b, ssm_c_re_b, ssm_c_im_b)
        w1, cmat, decay = _ssm_operators([p[layer].astype(F32) for p in fw],
                                         [p[layer].astype(F32) for p in bw],
                                         ssm_d[layer].astype(F32), tc, ncs)
        y_g = _ssm(u_g, w1, cmat, decay, ncs, bsz)
        y_gelu = y_g.reshape(g_n, ncs, nseg, bsz, tc, SSM_GROUP).transpose(3, 2, 1, 4, 0, 5).reshape(tp, ssm_w)

        lam_init = 0.8 - 0.6 * math.exp(-0.3 * layer)
        lam = (jnp.exp(jnp.sum(diff_lam_q1[layer].astype(F32) * diff_lam_k1[layer].astype(F32)))
               - jnp.exp(jnp.sum(diff_lam_q2[layer].astype(F32) * diff_lam_k2[layer].astype(F32)))
               + lam_init)
        o_attn = _attention(proj, slopes, lam.reshape(1), diff_subln_g[layer][None].astype(F32),
                            bsz, lp, seq, ssm_w, lam_init)

        merged = _merge(y_gelu, o_attn, proj, w_ssm_glu[layer].astype(BF16),
                        w_attn_out[layer].astype(BF16), ssm_w + 2 * qk_w + v_w, d)

        h1, hn, ridx, rgate, rcnt = _outproj_router(
            merged, h, w_out[layer].astype(BF16), norm_moe_g[layer][None].astype(F32),
            w_router[layer].astype(F32), b_router[layer][None].astype(F32), lp, seq, bsz)

        e_idx = ridx[:, :TOP_K]
        rank = ridx[:, TOP_K:2 * TOP_K]
        counts = rcnt[0, :n_e].astype(jnp.int32)
        padded = ((counts + pb - 1) // pb) * pb
        pad_end = jnp.cumsum(padded)
        pad_start = pad_end - padded
        n_blocks = -(-(bsz * seq * TOP_K) // pb) + n_e
        n_slots = n_blocks * pb
        slot = pad_start[e_idx] + rank
        tok_ids = jnp.broadcast_to(jnp.arange(tp, dtype=jnp.int32)[:, None], (tp, TOP_K))
        slot_tok = jnp.zeros((n_slots,), jnp.int32).at[
            jnp.where(tok_valid[:, None], slot, n_slots).reshape(-1)].set(tok_ids.reshape(-1), mode='drop')
        n_used = (pad_end[-1] // pb).astype(jnp.int32)
        blk_map = jnp.minimum(jnp.arange(n_blocks, dtype=jnp.int32), n_used - 1)
        blk_expert = jnp.minimum(jnp.searchsorted(pad_end, blk_map * pb, side='right'), n_e - 1).astype(jnp.int32)
        n_used1 = n_used.reshape(1)

        w1p = w_exp1[layer].reshape(n_e, d, f_ff, 2).transpose(0, 3, 1, 2).astype(BF16)
        b1p = b_exp1[layer].astype(F32).reshape(n_e, f_ff, 2).transpose(0, 2, 1)

        xs = _dispatch(slot_tok, n_used1, hn, n_blocks, pb)
        act = _ffn1(blk_expert, blk_map, n_used1, xs, w1p, b1p, pb)
        ys = _ffn2(blk_expert, blk_map, n_used1, act, w_exp2[layer].astype(BF16),
                   b_exp2[layer].astype(F32)[:, None, :], pb)

        slot_flat = jnp.where(tok_valid[:, None], slot, 0).reshape(-1)
        g_last = norm_final_g if layer == depth - 1 else jnp.ones((d,), F32)
        h = _combine(slot_flat, ys, h1, rgate, g_last[None].astype(F32))
    assert depth == 1
    return h.reshape(bsz, lp, d)[:, N_META:seq]
```

```python
import functools
import math

import jax
import jax.numpy as jnp
from jax import lax
from jax.experimental import pallas as pl
from jax.experimental.pallas import tpu as pltpu

F32 = jnp.float32
BF16 = jnp.bfloat16

N_META = 16
NORM_EPS = 1e-5
SSM_GROUP = 16
SSM_STATE = 64
ATTN_HEADS = 8
ATTN_HEAD_DIM = 64
ATTN_V_DIM = 2 * ATTN_HEAD_DIM
ALIBI_MAX_EXP = 8.0
ALIBI_SPLIT_BITS = 7
ALIBI_SPLIT = 1 << ALIBI_SPLIT_BITS
TOP_K = 4
SWIGLU_LIMIT = 7.0
SWIGLU_ALPHA = 1.702

LANES = 128
MXU_DIM = 256
ATTN_TK = 3 * MXU_DIM
ATTN_TQ = ATTN_TK // 2
ATTN_UNROLL = 2
SSM_CHUNK = 32
MOE_BLOCK = 512
VMEM_LIMIT = 56 * 1024 * 1024
NEG_BIG = -1e30
HIGHEST = lax.Precision.HIGHEST


def _div_tile(n, target, mult):
    best = None
    for t in range(mult, min(n, target) + 1, mult):
        if n % t == 0:
            best = t
    assert best is not None, (n, target, mult)
    return best


def _params(n_axes):
    return pltpu.CompilerParams(dimension_semantics=("arbitrary",) * n_axes,
                                vmem_limit_bytes=VMEM_LIMIT)


def _rms(x, g):
    return x * lax.rsqrt(jnp.mean(x * x, axis=-1, keepdims=True) + NORM_EPS) * g


def _inproj_kernel(h_ref, g_ref, w_ref, o_ref, xn_ref):
    @pl.when(pl.program_id(1) == 0)
    def _():
        xn_ref[...] = _rms(h_ref[...], g_ref[...]).astype(BF16)

    o_ref[...] = jnp.dot(xn_ref[...], w_ref[...], preferred_element_type=F32).astype(o_ref.dtype)


def _inproj(h, g, w_bf16):
    tp, d = h.shape
    n = w_bf16.shape[1]
    tm = _div_tile(tp, 640, LANES)
    tn = _div_tile(n, 1024, LANES)
    return pl.pallas_call(
        _inproj_kernel,
        grid=(tp // tm, n // tn),
        in_specs=[pl.BlockSpec((tm, d), lambda i, j: (i, 0)),
                  pl.BlockSpec((1, d), lambda i, j: (0, 0)),
                  pl.BlockSpec((d, tn), lambda i, j: (0, j))],
        out_specs=pl.BlockSpec((tm, tn), lambda i, j: (i, j)),
        out_shape=jax.ShapeDtypeStruct((tp, n), BF16),
        scratch_shapes=[pltpu.VMEM((tm, d), BF16)],
        compiler_params=_params(2),
        name="inproj",
    )(h, g, w_bf16)


def _ssm_direction_terms(lam_re, lam_im, log_dt, b_re, b_im):
    dt = jnp.exp(log_dt)[:, None]
    zr = lam_re * dt
    zi = lam_im * dt

    def apow(n):
        n = n.astype(F32)[:, None, None]
        mag = jnp.exp(n * zr[None])
        return mag * jnp.cos(n * zi[None]), mag * jnp.sin(n * zi[None])

    ar, ai = apow(jnp.ones((1,), F32))
    ar, ai = ar[0], ai[0]
    den = lam_re * lam_re + lam_im * lam_im
    fr = ((ar - 1.0) * lam_re + ai * lam_im) / den
    fi = (ai * lam_re - (ar - 1.0) * lam_im) / den
    bbr = fr[..., None] * b_re - fi[..., None] * b_im
    bbi = fr[..., None] * b_im + fi[..., None] * b_re
    return apow, bbr, bbi


def _pad_last(x, n):
    return jnp.pad(x, [(0, 0)] * (x.ndim - 1) + [(0, n - x.shape[-1])])


def _ssm_operators(fw, bw, ssm_d, tc, ncs):
    g_n, p_n = fw[0].shape
    io = SSM_GROUP
    n = jnp.arange(tc)
    terms = []
    for (lam_re, lam_im, log_dt, b_re, b_im, c_re, c_im), rev in ((fw, False), (bw, True)):
        apow, bbr, bbi = _ssm_direction_terms(lam_re, lam_im, log_dt, b_re, b_im)
        er, ei = apow(n)
        mr = c_re[None] * er[:, :, None, :] - c_im[None] * ei[:, :, None, :]
        mi = c_re[None] * ei[:, :, None, :] + c_im[None] * er[:, :, None, :]
        kern = (jnp.einsum('ngop,gpi->ngoi', mr, bbr, precision=HIGHEST)
                - jnp.einsum('ngop,gpi->ngoi', mi, bbi, precision=HIGHEST))
        es_r, es_i = (er, ei) if rev else (er[::-1], ei[::-1])
        st_re = es_r[..., None] * bbr[None] - es_i[..., None] * bbi[None]
        st_im = es_r[..., None] * bbi[None] + es_i[..., None] * bbr[None]
        st = [_pad_last(x.transpose(1, 0, 3, 2).reshape(g_n, tc * io, p_n), LANES)
              for x in (st_re, st_im)]
        cr_, ci_ = apow((tc - n) if rev else (n + 1))
        rr = c_re[None] * cr_[:, :, None, :] - c_im[None] * ci_[:, :, None, :]
        ri = c_re[None] * ci_[:, :, None, :] + c_im[None] * cr_[:, :, None, :]
        ro = [jnp.pad(x.transpose(1, 3, 0, 2).reshape(g_n, p_n, tc * io),
                      ((0, 0), (0, LANES - p_n), (0, 0))) for x in (rr, -ri)]
        dr, di = apow(jnp.array([tc, tc * ncs]))
        dec = [_pad_last(dr, LANES), _pad_last(di, LANES)]
        terms.append((kern, st, ro, dec))
    (kf, stf, rof, decf), (kb, stb, rob, decb) = terms
    dmat = jnp.eye(io, dtype=F32)[None] * ssm_d.reshape(g_n, io)[:, :, None]
    kcat = jnp.concatenate([kb[:0:-1], kf[:1] + kb[:1] + dmat[None], kf[1:]], axis=0)
    idx = (n[None, :] - n[:, None]) + (tc - 1)
    toep = kcat[idx]
    toep = toep.transpose(2, 0, 4, 1, 3).reshape(g_n, tc * io, tc * io)
    w1 = jnp.concatenate([toep] + stf + stb, axis=-1).astype(BF16)
    cmat = jnp.concatenate(rof + rob, axis=1).astype(BF16)
    chunk_seg = [jnp.stack([decf[0][k], decf[1][k], decb[0][k], decb[1][k]], axis=1) for k in (0, 1)]
    decay = jnp.concatenate(chunk_seg, axis=1)
    return w1, cmat, decay


def _gelu_tanh(y):
    return 0.5 * y * (1.0 + jnp.tanh(math.sqrt(2.0 / math.pi) * (y + 0.044715 * (y * y * y))))


def _cmul(ar, ai, xr, xi):
    return ar * xr - ai * xi, ar * xi + ai * xr


def _ssm_kernel(u_ref, w_ref, c_ref, dec_ref, o_ref, s_ref, h_ref, *, ncs, nb, n_toep):
    z = jnp.dot(u_ref[0], w_ref[0], preferred_element_type=F32)
    s_ref[...] = z[:, n_toep:]
    dec = dec_ref[0]
    afr, afi, abr, abi = dec[0:1], dec[1:2], dec[2:3], dec[3:4]
    sfr, sfi, sbr, sbi = dec[4:5], dec[5:6], dec[6:7], dec[7:8]
    lf_re, lf_im = slice(0, LANES), slice(LANES, 2 * LANES)
    lb_re, lb_im = slice(2 * LANES, 3 * LANES), slice(3 * LANES, 4 * LANES)

    def tiles(c):
        return (pl.ds(pl.multiple_of(c * 8, 8), 8), pl.ds(pl.multiple_of((ncs - 1 - c) * 8, 8), 8))

    def local(c, carry):
        hfr, hfi, hbr, hbi = carry
        rf, rb = tiles(c)
        h_ref[rf, lf_re] = hfr
        h_ref[rf, lf_im] = hfi
        h_ref[rb, lb_re] = hbr
        h_ref[rb, lb_im] = hbi
        nfr, nfi = _cmul(afr, afi, hfr, hfi)
        nbr, nbi = _cmul(abr, abi, hbr, hbi)
        return (nfr + s_ref[rf, lf_re], nfi + s_ref[rf, lf_im],
                nbr + s_ref[rb, lb_re], nbi + s_ref[rb, lb_im])

    zero = jnp.zeros((8, LANES), F32)
    efr, efi, ebr, ebi = lax.fori_loop(0, ncs, local, (zero, zero, zero, zero))

    row8 = lax.broadcasted_iota(jnp.int32, (8, LANES), 0)

    def shift_dn(x):
        return jnp.where(row8 >= nb, pltpu.roll(x, nb, 0), 0.0)

    def shift_up(x):
        return jnp.where(row8 < 8 - nb, pltpu.roll(x, 8 - nb, 0), 0.0)

    def segment_in(er, ei, ar, ai, shift):
        xr, xi = shift(er), shift(ei)
        ir, ii = xr, xi
        for _ in range(8 // nb - 2):
            tr, ti = _cmul(ar, ai, shift(ir), shift(ii))
            ir, ii = xr + tr, xi + ti
        return ir, ii

    ifr, ifi = segment_in(efr, efi, sfr, sfi, shift_dn)
    ibr, ibi = segment_in(ebr, ebi, sbr, sbi, shift_up)

    def fix(c, carry):
        pfr, pfi, pbr, pbi = carry
        rf, rb = tiles(c)
        dfr, dfi = _cmul(pfr, pfi, ifr, ifi)
        dbr, dbi = _cmul(pbr, pbi, ibr, ibi)
        h_ref[rf, lf_re] = h_ref[rf, lf_re] + dfr
        h_ref[rf, lf_im] = h_ref[rf, lf_im] + dfi
        h_ref[rb, lb_re] = h_ref[rb, lb_re] + dbr
        h_ref[rb, lb_im] = h_ref[rb, lb_im] + dbi
        return _cmul(afr, afi, pfr, pfi) + _cmul(abr, abi, pbr, pbi)

    one = jnp.ones((8, LANES), F32)
    lax.fori_loop(0, ncs, fix, (one, zero, one, zero))

    y = z[:, :n_toep] + jnp.dot(h_ref[...].astype(BF16), c_ref[0], preferred_element_type=F32)
    o_ref[0] = _gelu_tanh(y).astype(o_ref.dtype)


def _ssm(u_g, w1, cmat, decay, ncs, nb):
    g_n, rows, n_toep = u_g.shape
    n_w = w1.shape[-1]
    n_st = n_w - n_toep
    return pl.pallas_call(
        functools.partial(_ssm_kernel, ncs=ncs, nb=nb, n_toep=n_toep),
        grid=(g_n,),
        in_specs=[pl.BlockSpec((1, rows, n_toep), lambda g: (g, 0, 0)),
                  pl.BlockSpec((1, n_toep, n_w), lambda g: (g, 0, 0)),
                  pl.BlockSpec((1, n_st, n_toep), lambda g: (g, 0, 0)),
                  pl.BlockSpec((1, 8, LANES), lambda g: (g, 0, 0))],
        out_specs=pl.BlockSpec((1, rows, n_toep), lambda g: (g, 0, 0)),
        out_shape=jax.ShapeDtypeStruct((g_n, rows, n_toep), BF16),
        scratch_shapes=[pltpu.VMEM((rows, n_st), F32), pltpu.VMEM((rows, n_st), F32)],
        compiler_params=_params(1),
        name="ssm",
    )(u_g, w1, cmat, decay)


def _key_features(lp, seq):
    pos = jnp.arange(lp, dtype=jnp.int32)
    live = (pos >= N_META).astype(F32)
    cols = [live, live,
            (pos >> ALIBI_SPLIT_BITS).astype(F32) * live,
            (pos & (ALIBI_SPLIT - 1)).astype(F32) * live,
            (pos >= seq).astype(F32)]
    return _pad_last(jnp.stack(cols, axis=1), LANES).astype(BF16)


def _attn_kernel(slope_ref, lam_ref, q_ref, k_ref, v_ref, kf_ref, g_ref, o_ref, *, tq, tk, lam_init):
    slope = slope_ref[pl.program_id(1)]
    lam = lam_ref[0]
    q0 = pl.program_id(2) * tq
    nk = k_ref.shape[0] // tk
    q = q_ref[...] * jnp.asarray(ATTN_HEAD_DIM ** -0.5, BF16)
    lane = lax.broadcasted_iota(jnp.int32, q.shape, 1)
    zero = jnp.zeros_like(q)
    q_maps = (jnp.where(lane < ATTN_HEAD_DIM, q, zero),
              jnp.where(lane >= ATTN_HEAD_DIM, q, zero))

    qpos = q0 + lax.broadcasted_iota(jnp.int32, q.shape, 0)
    live = jnp.where(qpos >= N_META, 1.0, 0.0)
    hi = lax.shift_right_logical(qpos, ALIBI_SPLIT_BITS).astype(F32) * (slope * ALIBI_SPLIT) * live
    lo = (qpos & (ALIBI_SPLIT - 1)).astype(F32) * slope * live
    dist_cols = jnp.where(lane == 0, -hi, jnp.where(lane == 1, -lo, jnp.where(
        lane == 2, slope * ALIBI_SPLIT * live, jnp.where(lane == 3, slope * live, 0.0))))
    pad_col = jnp.where(lane == 4, NEG_BIG, 0.0)
    f_left = (dist_cols + pad_col).astype(BF16)
    f_right = (pad_col - dist_cols).astype(BF16)

    nt = (((1,), (1,)), ((), ()))
    ones = jnp.ones((tk, LANES), BF16)

    def step(j, carry, feats):
        rows = pl.ds(pl.multiple_of(j * tk, tk), tk)
        kx = jnp.concatenate([k_ref[rows, :], kf_ref[rows, :]], axis=1)
        vx = jnp.concatenate([v_ref[rows, :], ones], axis=1)
        out = []
        for mp in (0, 1):
            m, acc = carry[2 * mp], carry[2 * mp + 1]
            s = None
            for f in feats:
                sf = lax.dot_general(jnp.concatenate([q_maps[mp], f], axis=1), kx, nt,
                                     preferred_element_type=F32)
                s = sf if s is None else jnp.minimum(s, sf)
            m_new = jnp.maximum(m, jnp.max(s, axis=-1, keepdims=True))
            p = jnp.exp(s - m_new).astype(BF16)
            acc = jnp.exp(m - m_new) * acc + jnp.dot(p, vx, preferred_element_type=F32)
            out += [m_new, acc]
        return tuple(out)

    diag = q0 // tk

    def off_diag(j, carry):
        jj = jnp.where(j >= diag, j + 1, j)
        return step(jj, carry, (jnp.where(jj < diag, f_left, f_right),))

    m0 = jnp.full((tq, 1), NEG_BIG, F32)
    a0 = jnp.zeros((tq, 2 * LANES), F32)
    carry = lax.fori_loop(0, nk - 1, off_diag, (m0, a0, m0, a0), unroll=ATTN_UNROLL)
    _, a1, _, a2 = step(diag, carry, (f_left, f_right))
    o = (a1[:, :ATTN_V_DIM] / a1[:, ATTN_V_DIM:ATTN_V_DIM + 1]
         - lam * (a2[:, :ATTN_V_DIM] / a2[:, ATTN_V_DIM:ATTN_V_DIM + 1]))
    o_ref[...] = (_rms(o, g_ref[...]) * (1.0 - lam_init)).astype(o_ref.dtype)


def _attention(proj, slopes, lam, subln_g, bsz, lp, seq, col0, lam_init):
    tp = proj.shape[0]
    tk, tq = ATTN_TK, ATTN_TQ
    assert lp % tk == 0 and tk % tq == 0
    nq = lp // tq
    qc, kc, vc = (col0 // LANES + i * ATTN_HEADS for i in range(3))
    fixed = lambda b, h, i: (0, 0)
    return pl.pallas_call(
        functools.partial(_attn_kernel, tq=tq, tk=tk, lam_init=lam_init),
        grid=(bsz, ATTN_HEADS, nq),
        in_specs=[pl.BlockSpec(memory_space=pltpu.SMEM),
                  pl.BlockSpec(memory_space=pltpu.SMEM),
                  pl.BlockSpec((tq, LANES), lambda b, h, i: (b * nq + i, qc + h)),
                  pl.BlockSpec((lp, LANES), lambda b, h, i: (b, kc + h)),
                  pl.BlockSpec((lp, LANES), lambda b, h, i: (b, vc + h)),
                  pl.BlockSpec((lp, LANES), fixed),
                  pl.BlockSpec((1, LANES), fixed)],
        out_specs=pl.BlockSpec((tq, LANES), lambda b, h, i: (b * nq + i, h)),
        out_shape=jax.ShapeDtypeStruct((tp, ATTN_HEADS * ATTN_V_DIM), BF16),
        compiler_params=_params(3),
        name="attn",
    )(slopes, lam, proj, proj, proj, _key_features(lp, seq), subln_g)


def _merge_kernel(y_ref, o_ref, gs_ref, ga_ref, wa_ref, wb_ref, wo_ref, out_ref):
    y = y_ref[...]
    glu_a = jnp.dot(y, wa_ref[...], preferred_element_type=F32)
    glu_b = jnp.dot(y, wb_ref[...], preferred_element_type=F32)
    y_attn = jnp.dot(o_ref[...], wo_ref[...], preferred_element_type=F32)
    y_ssm = glu_a * jax.nn.sigmoid(glu_b)
    merged = (jax.nn.sigmoid(gs_ref[...].astype(F32)) * y_ssm
              + jax.nn.sigmoid(ga_ref[...].astype(F32)) * y_attn)
    out_ref[...] = merged.astype(out_ref.dtype)


def _merge(y_gelu, o_attn, proj, w_glu, w_ao, gate_col0, d):
    tp, w = y_gelu.shape
    tm = _div_tile(tp, 640, LANES)
    tn = _div_tile(d, 512, LANES)
    nj = d // tn
    gs0 = gate_col0 // tn
    return pl.pallas_call(
        _merge_kernel,
        grid=(tp // tm, nj),
        in_specs=[pl.BlockSpec((tm, w), lambda i, j: (i, 0)),
                  pl.BlockSpec((tm, o_attn.shape[1]), lambda i, j: (i, 0)),
                  pl.BlockSpec((tm, tn), lambda i, j: (i, gs0 + j)),
                  pl.BlockSpec((tm, tn), lambda i, j: (i, gs0 + nj + j)),
                  pl.BlockSpec((w, tn), lambda i, j: (0, j)),
                  pl.BlockSpec((w, tn), lambda i, j: (0, nj + j)),
                  pl.BlockSpec((w_ao.shape[0], tn), lambda i, j: (0, j))],
        out_specs=pl.BlockSpec((tm, tn), lambda i, j: (i, j)),
        out_shape=jax.ShapeDtypeStruct((tp, d), BF16),
        compiler_params=_params(2),
        name="merge",
    )(y_gelu, o_attn, proj, proj, w_glu, w_glu, w_ao)


def _outproj_router_kernel(m_ref, h_ref, wo_ref, g_ref, wr_ref, br_ref,
                           h1_ref, hn_ref, idx_ref, gate_ref, cnt_ref, *, tm, lp, seq, bsz):
    i = pl.program_id(0)
    n_e = wr_ref.shape[1]

    @pl.when(i == 0)
    def _():
        cnt_ref[...] = jnp.zeros_like(cnt_ref)

    h1 = h_ref[...] + jnp.dot(m_ref[...], wo_ref[...], preferred_element_type=F32)
    h1_ref[...] = h1
    hn = _rms(h1, g_ref[...])
    hn_ref[...] = hn
    logits = jnp.dot(hn, wr_ref[...], preferred_element_type=F32, precision=HIGHEST) + br_ref[...]

    row = i * tm + lax.broadcasted_iota(jnp.int32, (tm, 1), 0)
    valid = row < 0
    for b in range(bsz):
        valid = valid | ((row >= b * lp) & (row < b * lp + seq))

    lane = lax.broadcasted_iota(jnp.int32, (tm, n_e), 1).astype(F32)
    work = logits
    vals, idxs, hots = [], [], []
    for _ in range(TOP_K):
        mk = jnp.max(work, axis=-1, keepdims=True)
        ik = jnp.min(jnp.where(work == mk, lane, float(n_e)), axis=-1, keepdims=True)
        hot = lane == ik
        work = jnp.where(hot, -jnp.inf, work)
        vals.append(mk)
        idxs.append(ik)
        hots.append(hot)
    exps = [jnp.exp(v - vals[0]) for v in vals]
    denom = exps[0]
    for e in exps[1:]:
        denom = denom + e

    multi = jnp.zeros((tm, n_e), F32)
    for hot in hots:
        multi = multi + jnp.where(hot & valid, 1.0, 0.0)
    r_i = lax.broadcasted_iota(jnp.int32, (tm, tm), 0)
    c_i = lax.broadcasted_iota(jnp.int32, (tm, tm), 1)
    tri = jnp.where(r_i > c_i, 1.0, 0.0).astype(BF16)
    before = jnp.dot(tri, multi.astype(BF16), preferred_element_type=F32) + cnt_ref[0:1, 0:n_e]

    lane_o = lax.broadcasted_iota(jnp.int32, (tm, LANES), 1)
    idx_out = jnp.zeros((tm, LANES), jnp.int32)
    gate_out = jnp.zeros((tm, LANES), F32)
    for k in range(TOP_K):
        rank = jnp.sum(jnp.where(hots[k], before, 0.0), axis=-1, keepdims=True).astype(jnp.int32)
        idx_out = jnp.where(lane_o == k, idxs[k].astype(jnp.int32), idx_out)
        idx_out = jnp.where(lane_o == TOP_K + k, rank, idx_out)
        gate_out = jnp.where(lane_o == k, exps[k] / denom, gate_out)
    idx_ref[...] = idx_out
    gate_ref[...] = gate_out
    cnt_ref[0:1, 0:n_e] = cnt_ref[0:1, 0:n_e] + jnp.sum(multi, axis=0, keepdims=True)


def _outproj_router(merged, h, w_out, g, w_router, b_router, lp, seq, bsz):
    tp, d = h.shape
    n_e = w_router.shape[1]
    tm = _div_tile(tp, 256, LANES)
    row = lambda i: (i, 0)
    fixed = lambda i: (0, 0)
    return pl.pallas_call(
        functools.partial(_outproj_router_kernel, tm=tm, lp=lp, seq=seq, bsz=bsz),
        grid=(tp // tm,),
        in_specs=[pl.BlockSpec((tm, d), row), pl.BlockSpec((tm, d), row),
                  pl.BlockSpec((d, d), fixed), pl.BlockSpec((1, d), fixed),
                  pl.BlockSpec((d, n_e), fixed), pl.BlockSpec((1, n_e), fixed)],
        out_specs=[pl.BlockSpec((tm, d), row), pl.BlockSpec((tm, d), row),
                   pl.BlockSpec((tm, LANES), row), pl.BlockSpec((tm, LANES), row),
                   pl.BlockSpec((8, LANES), fixed)],
        out_shape=[jax.ShapeDtypeStruct((tp, d), F32), jax.ShapeDtypeStruct((tp, d), F32),
                   jax.ShapeDtypeStruct((tp, LANES), jnp.int32), jax.ShapeDtypeStruct((tp, LANES), F32),
                   jax.ShapeDtypeStruct((8, LANES), F32)],
        compiler_params=_params(1),
        name="outproj_router",
    )(merged, h, w_out, g, w_router, b_router)


def _row_copy(src_hbm, row, dst, dst_row, sem):
    return pltpu.make_async_copy(src_hbm.at[pl.ds(row, 1), :], dst.at[pl.ds(dst_row, 1), :], sem)


def _dispatch_kernel(slot_ref, hn_ref, xs_in, xs_hbm, sem, *, tm, n_slots):
    del xs_in
    i = pl.program_id(0)

    def copies(r):
        base = (i * tm + r) * TOP_K
        routed = slot_ref[base] < n_slots
        return routed, [pltpu.make_async_copy(hn_ref.at[pl.ds(r, 1), :],
                                              xs_hbm.at[pl.ds(slot_ref[base + k], 1), :], sem)
                        for k in range(TOP_K)]

    def issue(r, c):
        routed, cps = copies(r)

        @pl.when(routed)
        def _():
            for cp in cps:
                cp.start()
        return c

    def drain(r, c):
        routed, cps = copies(r)

        @pl.when(routed)
        def _():
            for cp in cps:
                cp.wait()
        return c

    lax.fori_loop(0, tm, issue, 0)
    lax.fori_loop(0, tm, drain, 0)


def _dispatch(slot_flat, hn, n_slots):
    tp, d = hn.shape
    tm = _div_tile(tp, 256, LANES)
    return pl.pallas_call(
        functools.partial(_dispatch_kernel, tm=tm, n_slots=n_slots),
        grid_spec=pltpu.PrefetchScalarGridSpec(
            num_scalar_prefetch=1,
            grid=(tp // tm,),
            in_specs=[pl.BlockSpec((tm, d), lambda i, s: (i, 0)),
                      pl.BlockSpec(memory_space=pl.ANY)],
            out_specs=pl.BlockSpec(memory_space=pl.ANY),
            scratch_shapes=[pltpu.SemaphoreType.DMA]),
        out_shape=jax.ShapeDtypeStruct((n_slots, d), hn.dtype),
        input_output_aliases={2: 0},
        compiler_params=_params(1),
        name="dispatch",
    )(slot_flat, hn, jnp.zeros((n_slots, d), hn.dtype))


def _w1_prep_kernel(w_ref, o_ref):
    r = lax.broadcasted_iota(jnp.int32, (MXU_DIM, MXU_DIM), 0)
    c = lax.broadcasted_iota(jnp.int32, (MXU_DIM, MXU_DIM), 1)
    sel = jnp.where(r == jnp.where(c < LANES, 2 * c, 2 * (c - LANES) + 1), 1.0, 0.0).astype(BF16)
    for g in range(w_ref.shape[2] // MXU_DIM):
        cols = slice(g * MXU_DIM, (g + 1) * MXU_DIM)
        o_ref[0, :, cols] = jnp.dot(w_ref[0, :, cols].astype(BF16), sel,
                                    preferred_element_type=F32).astype(BF16)


def _w1_prep(w1):
    n_e, d, f2 = w1.shape
    tw = _div_tile(f2, 1024, MXU_DIM)
    spec = pl.BlockSpec((1, d, tw), lambda e, j: (e, 0, j))
    return pl.pallas_call(
        _w1_prep_kernel,
        grid=(n_e, f2 // tw),
        in_specs=[spec],
        out_specs=spec,
        out_shape=jax.ShapeDtypeStruct(w1.shape, BF16),
        compiler_params=_params(2),
        name="w1_prep",
    )(w1)


def _ffn1_kernel(be_ref, bm_ref, nused_ref, x_ref, w_ref, b_ref, o_ref):
    @pl.when(pl.program_id(1) < nused_ref[0])
    def _():
        h = jnp.dot(x_ref[...].astype(BF16), w_ref[0], preferred_element_type=F32) + b_ref[0]
        for g in range(h.shape[1] // MXU_DIM):
            glu = jnp.minimum(h[:, g * MXU_DIM:g * MXU_DIM + LANES], SWIGLU_LIMIT)
            lin = jnp.clip(h[:, g * MXU_DIM + LANES:(g + 1) * MXU_DIM], -SWIGLU_LIMIT, SWIGLU_LIMIT)
            o_ref[:, g * LANES:(g + 1) * LANES] = (
                glu * jax.nn.sigmoid(SWIGLU_ALPHA * glu) * (lin + 1.0)).astype(o_ref.dtype)

    @pl.when(pl.program_id(1) >= nused_ref[0])
    def _():
        o_ref[...] = jnp.zeros_like(o_ref)


def _ffn1(blk_expert, blk_map, n_used, xs, w1g, b1g, pb):
    n_slots, d = xs.shape
    f = w1g.shape[-1] // 2
    tn = _div_tile(f, 1024, LANES)
    return pl.pallas_call(
        _ffn1_kernel,
        grid_spec=pltpu.PrefetchScalarGridSpec(
            num_scalar_prefetch=3,
            grid=(f // tn, n_slots // pb),
            in_specs=[pl.BlockSpec((pb, d), lambda j, i, be, bm, nu: (bm[i], 0)),
                      pl.BlockSpec((1, d, 2 * tn), lambda j, i, be, bm, nu: (be[i], 0, j)),
                      pl.BlockSpec((1, 1, 2 * tn), lambda j, i, be, bm, nu: (be[i], 0, j))],
            out_specs=pl.BlockSpec((pb, tn), lambda j, i, be, bm, nu: (i, j))),
        out_shape=jax.ShapeDtypeStruct((n_slots, f), BF16),
        compiler_params=_params(2),
        name="ffn1",
    )(blk_expert, blk_map, n_used, xs, w1g, b1g)


def _ffn2_kernel(be_ref, bm_ref, nused_ref, a_ref, w_ref, b_ref, o_ref, wb_ref):
    i = pl.program_id(1)

    @pl.when((i == 0) | (be_ref[i] != be_ref[jnp.maximum(i - 1, 0)]))
    def _():
        wb_ref[...] = w_ref[0].astype(BF16)

    @pl.when(i < nused_ref[0])
    def _():
        o_ref[...] = jnp.dot(a_ref[...], wb_ref[...], preferred_element_type=F32) + b_ref[0]

    @pl.when(i >= nused_ref[0])
    def _():
        o_ref[...] = jnp.zeros_like(o_ref)


def _ffn2(blk_expert, blk_map, n_used, act, w2, b2, pb):
    n_slots, f = act.shape
    d = w2.shape[-1]
    tn = _div_tile(d, 1024, LANES)
    return pl.pallas_call(
        _ffn2_kernel,
        grid_spec=pltpu.PrefetchScalarGridSpec(
            num_scalar_prefetch=3,
            grid=(d // tn, n_slots // pb),
            in_specs=[pl.BlockSpec((pb, f), lambda j, i, be, bm, nu: (bm[i], 0)),
                      pl.BlockSpec((1, f, tn), lambda j, i, be, bm, nu: (be[i], 0, j)),
                      pl.BlockSpec((1, 1, tn), lambda j, i, be, bm, nu: (be[i], 0, j))],
            out_specs=pl.BlockSpec((pb, tn), lambda j, i, be, bm, nu: (i, j)),
            scratch_shapes=[pltpu.VMEM((f, tn), BF16)]),
        out_shape=jax.ShapeDtypeStruct((n_slots, d), F32),
        compiler_params=_params(2),
        name="ffn2",
    )(blk_expert, blk_map, n_used, act, w2, b2)


def _combine_kernel(slot_ref, ys_hbm, h1_ref, gate_ref, g_ref, o_ref, buf, sem, *, tm):
    i = pl.program_id(0)

    def issue(r, c):
        for k in range(TOP_K):
            _row_copy(ys_hbm, slot_ref[(i * tm + r) * TOP_K + k], buf.at[k], r, sem).start()
        return c

    lax.fori_loop(0, tm, issue, 0)

    def drain(r, c):
        for k in range(TOP_K):
            _row_copy(ys_hbm, 0, buf.at[k], r, sem).wait()
        return c

    lax.fori_loop(0, tm, drain, 0)
    gates = gate_ref[...]
    h2 = h1_ref[...]
    for k in range(TOP_K):
        h2 = h2 + gates[:, k:k + 1] * buf[k]
    o_ref[...] = _rms(h2, g_ref[...])


def _combine(slot_flat, ys, h1, gates, g):
    tp, d = h1.shape
    tm = _div_tile(tp, 128, LANES)
    row = lambda i, s: (i, 0)
    return pl.pallas_call(
        functools.partial(_combine_kernel, tm=tm),
        grid_spec=pltpu.PrefetchScalarGridSpec(
            num_scalar_prefetch=1,
            grid=(tp // tm,),
            in_specs=[pl.BlockSpec(memory_space=pl.ANY),
                      pl.BlockSpec((tm, d), row), pl.BlockSpec((tm, LANES), row),
                      pl.BlockSpec((1, d), lambda i, s: (0, 0))],
            out_specs=pl.BlockSpec((tm, d), row),
            scratch_shapes=[pltpu.VMEM((TOP_K, tm, d), F32), pltpu.SemaphoreType.DMA]),
        out_shape=jax.ShapeDtypeStruct((tp, d), F32),
        compiler_params=_params(1),
        name="combine",
    )(slot_flat, ys, h1, gates, g)


def kernel(x, meta_tokens, norm_mix_g, w_in, ssm_lam_re_f, ssm_lam_im_f, ssm_log_dt_f, ssm_b_re_f, ssm_b_im_f, ssm_c_re_f, ssm_c_im_f, ssm_lam_re_b, ssm_lam_im_b, ssm_log_dt_b, ssm_b_re_b, ssm_b_im_b, ssm_c_re_b, ssm_c_im_b, ssm_d, w_ssm_glu, diff_lam_q1, diff_lam_k1, diff_lam_q2, diff_lam_k2, diff_subln_g, w_attn_out, w_out, norm_moe_g, w_router, b_router, w_exp1, b_exp1, w_exp2, b_exp2, norm_final_g):
    bsz, s_len, d = x.shape
    depth = w_in.shape[0]
    assert depth == 1, "the combine kernel fuses the final RMSNorm, so exactly one layer is supported"
    seq = s_len + N_META
    lp = -(-seq // ATTN_TK) * ATTN_TK
    tp = bsz * lp
    ssm_w = ssm_d.shape[-1]
    g_n = ssm_w // SSM_GROUP
    qk_w = ATTN_HEADS * 2 * ATTN_HEAD_DIM
    v_w = ATTN_HEADS * ATTN_V_DIM
    n_e = w_router.shape[-1]
    f_ff = w_exp2.shape[2]
    tc = SSM_CHUNK
    nc = lp // tc
    assert 8 % bsz == 0, "the S5 scan packs batch x sequence segments into 8-row tiles"
    nseg = 8 // bsz
    assert nc % nseg == 0
    ncs = nc // nseg
    pb = MOE_BLOCK

    meta = jnp.broadcast_to(meta_tokens.astype(x.dtype)[None], (bsz, N_META, d))
    h = jnp.concatenate([meta, x, jnp.zeros((bsz, lp - seq, d), x.dtype)], axis=1).reshape(tp, d)

    pos = jnp.arange(tp, dtype=jnp.int32) % lp
    tok_valid = pos < seq
    slopes = jnp.exp2(-ALIBI_MAX_EXP * jnp.arange(1, ATTN_HEADS + 1, dtype=F32) / ATTN_HEADS)

    for layer in range(depth):
        proj = _inproj(h, norm_mix_g[layer][None], w_in[layer].astype(BF16))

        u_g = proj[:, :ssm_w].reshape(bsz, nseg, ncs, tc, g_n, SSM_GROUP)
        u_g = u_g.transpose(4, 2, 1, 0, 3, 5).reshape(g_n, nc * bsz, tc * SSM_GROUP)
        fw = (ssm_lam_re_f, ssm_lam_im_f, ssm_log_dt_f, ssm_b_re_f, ssm_b_im_f, ssm_c_re_f, ssm_c_im_f)
        bw = (ssm_lam_re_b, ssm_lam_im_b, ssm_log_dt_b, ssm_b_re_b, ssm_b_im_b, ssm_c_re_b, ssm_c_im_b)
        w1, cmat, decay = _ssm_operators([p[layer].astype(F32) for p in fw],
                                         [p[layer].astype(F32) for p in bw],
                                         ssm_d[layer].astype(F32), tc, ncs)
        y_g = _ssm(u_g, w1, cmat, decay, ncs, bsz)
        y_gelu = y_g.reshape(g_n, ncs, nseg, bsz, tc, SSM_GROUP).transpose(3, 2, 1, 4, 0, 5).reshape(tp, ssm_w)

        lam_init = 0.8 - 0.6 * math.exp(-0.3 * layer)
        lam = (jnp.exp(jnp.sum(diff_lam_q1[layer].astype(F32) * diff_lam_k1[layer].astype(F32)))
               - jnp.exp(jnp.sum(diff_lam_q2[layer].astype(F32) * diff_lam_k2[layer].astype(F32)))
               + lam_init)
        o_attn = _attention(proj, slopes, lam.reshape(1), diff_subln_g[layer][None].astype(F32),
                            bsz, lp, seq, ssm_w, lam_init)

        merged = _merge(y_gelu, o_attn, proj, w_ssm_glu[layer].astype(BF16),
                        w_attn_out[layer].astype(BF16), ssm_w + 2 * qk_w + v_w, d)

        h1, hn, ridx, rgate, rcnt = _outproj_router(
            merged, h, w_out[layer].astype(BF16), norm_moe_g[layer][None].astype(F32),
            w_router[layer].astype(F32), b_router[layer][None].astype(F32), lp, seq, bsz)

        e_idx = ridx[:, :TOP_K]
        rank = ridx[:, TOP_K:2 * TOP_K]
        counts = rcnt[0, :n_e].astype(jnp.int32)
        padded = ((counts + pb - 1) // pb) * pb
        pad_end = jnp.cumsum(padded)
        pad_start = pad_end - padded
        n_blocks = -(-(bsz * seq * TOP_K) // pb) + n_e
        n_slots = n_blocks * pb
        slot = pad_start[e_idx] + rank
        n_used = (pad_end[-1] // pb).astype(jnp.int32)
        blk_map = jnp.minimum(jnp.arange(n_blocks, dtype=jnp.int32), n_used - 1)
        blk_expert = jnp.minimum(jnp.sum(pad_end[None, :] <= (blk_map * pb)[:, None], axis=1), n_e - 1).astype(jnp.int32)
        n_used1 = n_used.reshape(1)

        w1g = _w1_prep(w_exp1[layer].astype(F32))
        b1g = (b_exp1[layer].astype(F32).reshape(n_e, 2 * f_ff // MXU_DIM, LANES, 2)
               .transpose(0, 1, 3, 2).reshape(n_e, 1, 2 * f_ff))

        xs = _dispatch(jnp.where(tok_valid[:, None], slot, n_slots).reshape(-1), hn, n_slots)
        act = _ffn1(blk_expert, blk_map, n_used1, xs, w1g, b1g, pb)
        ys = _ffn2(blk_expert, blk_map, n_used1, act, w_exp2[layer].astype(F32),
                   b_exp2[layer].astype(F32)[:, None, :], pb)

        slot_flat = jnp.where(tok_valid[:, None], slot, 0).reshape(-1)
        h = _combine(slot_flat, ys, h1, rgate, norm_final_g[None].astype(F32))
    return h.reshape(bsz, lp, d)[:, N_META:seq]
```

```python
import functools
import math

import jax
import jax.numpy as jnp
from jax import lax
from jax.experimental import pallas as pl
from jax.experimental.pallas import tpu as pltpu

F32 = jnp.float32
BF16 = jnp.bfloat16

N_META = 16
NORM_EPS = 1e-5
SSM_GROUP = 16
SSM_STATE = 64
ATTN_HEADS = 8
ATTN_HEAD_DIM = 64
ATTN_V_DIM = 2 * ATTN_HEAD_DIM
ALIBI_MAX_EXP = 8.0
ALIBI_SPLIT_BITS = 7
ALIBI_SPLIT = 1 << ALIBI_SPLIT_BITS
TOP_K = 4
SWIGLU_LIMIT = 7.0
SWIGLU_ALPHA = 1.702

LANES = 128
MXU_DIM = 256
ATTN_TK = 3 * MXU_DIM
ATTN_TQ = ATTN_TK // 2
ATTN_UNROLL = 5
DMA_UNROLL = 8
SSM_CHUNK = 16
MOE_BLOCK = 512
VMEM_LIMIT = 56 * 1024 * 1024
NEG_BIG = -1e30
HIGHEST = lax.Precision.HIGHEST


def _div_tile(n, target, mult):
    best = None
    for t in range(mult, min(n, target) + 1, mult):
        if n % t == 0:
            best = t
    assert best is not None, (n, target, mult)
    return best


def _params(n_axes):
    return pltpu.CompilerParams(dimension_semantics=("arbitrary",) * n_axes,
                                vmem_limit_bytes=VMEM_LIMIT)


def _rms(x, g):
    return x * lax.rsqrt(jnp.mean(x * x, axis=-1, keepdims=True) + NORM_EPS) * g


def _inproj_kernel(h_ref, g_ref, w_ref, o_ref, xn_ref):
    @pl.when(pl.program_id(1) == 0)
    def _():
        xn_ref[...] = _rms(h_ref[...], g_ref[...]).astype(BF16)

    o_ref[...] = jnp.dot(xn_ref[...], w_ref[...], preferred_element_type=F32).astype(o_ref.dtype)


def _inproj(h, g, w_bf16):
    tp, d = h.shape
    n = w_bf16.shape[1]
    tm = _div_tile(tp, 640, LANES)
    tn = _div_tile(n, 1024, LANES)
    return pl.pallas_call(
        _inproj_kernel,
        grid=(tp // tm, n // tn),
        in_specs=[pl.BlockSpec((tm, d), lambda i, j: (i, 0)),
                  pl.BlockSpec((1, d), lambda i, j: (0, 0)),
                  pl.BlockSpec((d, tn), lambda i, j: (0, j))],
        out_specs=pl.BlockSpec((tm, tn), lambda i, j: (i, j)),
        out_shape=jax.ShapeDtypeStruct((tp, n), BF16),
        scratch_shapes=[pltpu.VMEM((tm, d), BF16)],
        compiler_params=_params(2),
        name="inproj",
    )(h, g, w_bf16)


def _ssm_direction_terms(lam_re, lam_im, log_dt, b_re, b_im):
    dt = jnp.exp(log_dt)[:, None]
    zr = lam_re * dt
    zi = lam_im * dt

    def apow(n):
        n = n.astype(F32)[:, None, None]
        mag = jnp.exp(n * zr[None])
        return mag * jnp.cos(n * zi[None]), mag * jnp.sin(n * zi[None])

    ar, ai = apow(jnp.ones((1,), F32))
    ar, ai = ar[0], ai[0]
    den = lam_re * lam_re + lam_im * lam_im
    fr = ((ar - 1.0) * lam_re + ai * lam_im) / den
    fi = (ai * lam_re - (ar - 1.0) * lam_im) / den
    bbr = fr[..., None] * b_re - fi[..., None] * b_im
    bbi = fr[..., None] * b_im + fi[..., None] * b_re
    return apow, bbr, bbi


def _pad_last(x, n):
    return jnp.pad(x, [(0, 0)] * (x.ndim - 1) + [(0, n - x.shape[-1])])


def _ssm_operators(fw, bw, ssm_d, tc, ncs):
    g_n, p_n = fw[0].shape
    io = SSM_GROUP
    n = jnp.arange(tc)
    terms = []
    for (lam_re, lam_im, log_dt, b_re, b_im, c_re, c_im), rev in ((fw, False), (bw, True)):
        apow, bbr, bbi = _ssm_direction_terms(lam_re, lam_im, log_dt, b_re, b_im)
        er, ei = apow(n)
        mr = c_re[None] * er[:, :, None, :] - c_im[None] * ei[:, :, None, :]
        mi = c_re[None] * ei[:, :, None, :] + c_im[None] * er[:, :, None, :]
        kern = (jnp.einsum('ngop,gpi->ngoi', mr, bbr, precision=HIGHEST)
                - jnp.einsum('ngop,gpi->ngoi', mi, bbi, precision=HIGHEST))
        es_r, es_i = (er, ei) if rev else (er[::-1], ei[::-1])
        st_re = es_r[..., None] * bbr[None] - es_i[..., None] * bbi[None]
        st_im = es_r[..., None] * bbi[None] + es_i[..., None] * bbr[None]
        st = [_pad_last(x.transpose(1, 0, 3, 2).reshape(g_n, tc * io, p_n), LANES)
              for x in (st_re, st_im)]
        cr_, ci_ = apow((tc - n) if rev else (n + 1))
        rr = c_re[None] * cr_[:, :, None, :] - c_im[None] * ci_[:, :, None, :]
        ri = c_re[None] * ci_[:, :, None, :] + c_im[None] * cr_[:, :, None, :]
        ro = [jnp.pad(x.transpose(1, 3, 0, 2).reshape(g_n, p_n, tc * io),
                      ((0, 0), (0, LANES - p_n), (0, 0))) for x in (rr, -ri)]
        dr, di = apow(jnp.array([tc, tc * ncs]))
        dec = [_pad_last(dr, LANES), _pad_last(di, LANES)]
        terms.append((kern, st, ro, dec))
    (kf, stf, rof, decf), (kb, stb, rob, decb) = terms
    dmat = jnp.eye(io, dtype=F32)[None] * ssm_d.reshape(g_n, io)[:, :, None]
    kcat = jnp.concatenate([kb[:0:-1], kf[:1] + kb[:1] + dmat[None], kf[1:]], axis=0)
    idx = (n[None, :] - n[:, None]) + (tc - 1)
    toep = kcat[idx]
    toep = toep.transpose(2, 0, 4, 1, 3).reshape(g_n, tc * io, tc * io)
    w1 = jnp.concatenate([toep] + stf + stb, axis=-1).astype(BF16)
    cmat = jnp.concatenate(rof + rob, axis=1).astype(BF16)
    chunk_seg = [jnp.stack([decf[0][k], decf[1][k], decb[0][k], decb[1][k]], axis=1) for k in (0, 1)]
    decay = jnp.concatenate(chunk_seg, axis=1)
    return w1, cmat, decay


def _gelu_tanh(y):
    return 0.5 * y * (1.0 + jnp.tanh(math.sqrt(2.0 / math.pi) * (y + 0.044715 * (y * y * y))))


def _cmul(ar, ai, xr, xi):
    return ar * xr - ai * xi, ar * xi + ai * xr


def _ssm_kernel(u_ref, w_ref, c_ref, dec_ref, o_ref, s_ref, h_ref, *, ncs, nb, n_toep):
    z = jnp.dot(u_ref[0], w_ref[0], preferred_element_type=F32)
    s_ref[...] = z[:, n_toep:]
    dec = dec_ref[0]
    afr, afi, abr, abi = dec[0:1], dec[1:2], dec[2:3], dec[3:4]
    sfr, sfi, sbr, sbi = dec[4:5], dec[5:6], dec[6:7], dec[7:8]
    lf_re, lf_im = slice(0, LANES), slice(LANES, 2 * LANES)
    lb_re, lb_im = slice(2 * LANES, 3 * LANES), slice(3 * LANES, 4 * LANES)

    def tiles(c):
        return (pl.ds(pl.multiple_of(c * 8, 8), 8), pl.ds(pl.multiple_of((ncs - 1 - c) * 8, 8), 8))

    def local(c, carry):
        hfr, hfi, hbr, hbi = carry
        rf, rb = tiles(c)
        h_ref[rf, lf_re] = hfr
        h_ref[rf, lf_im] = hfi
        h_ref[rb, lb_re] = hbr
        h_ref[rb, lb_im] = hbi
        nfr, nfi = _cmul(afr, afi, hfr, hfi)
        nbr, nbi = _cmul(abr, abi, hbr, hbi)
        return (nfr + s_ref[rf, lf_re], nfi + s_ref[rf, lf_im],
                nbr + s_ref[rb, lb_re], nbi + s_ref[rb, lb_im])

    zero = jnp.zeros((8, LANES), F32)
    efr, efi, ebr, ebi = lax.fori_loop(0, ncs, local, (zero, zero, zero, zero))

    row8 = lax.broadcasted_iota(jnp.int32, (8, LANES), 0)

    def shift_dn(x):
        return jnp.where(row8 >= nb, pltpu.roll(x, nb, 0), 0.0)

    def shift_up(x):
        return jnp.where(row8 < 8 - nb, pltpu.roll(x, 8 - nb, 0), 0.0)

    def segment_in(er, ei, ar, ai, shift):
        xr, xi = shift(er), shift(ei)
        ir, ii = xr, xi
        for _ in range(8 // nb - 2):
            tr, ti = _cmul(ar, ai, shift(ir), shift(ii))
            ir, ii = xr + tr, xi + ti
        return ir, ii

    ifr, ifi = segment_in(efr, efi, sfr, sfi, shift_dn)
    ibr, ibi = segment_in(ebr, ebi, sbr, sbi, shift_up)

    def fix(c, carry):
        pfr, pfi, pbr, pbi = carry
        rf, rb = tiles(c)
        dfr, dfi = _cmul(pfr, pfi, ifr, ifi)
        dbr, dbi = _cmul(pbr, pbi, ibr, ibi)
        h_ref[rf, lf_re] = h_ref[rf, lf_re] + dfr
        h_ref[rf, lf_im] = h_ref[rf, lf_im] + dfi
        h_ref[rb, lb_re] = h_ref[rb, lb_re] + dbr
        h_ref[rb, lb_im] = h_ref[rb, lb_im] + dbi
        return _cmul(afr, afi, pfr, pfi) + _cmul(abr, abi, pbr, pbi)

    one = jnp.ones((8, LANES), F32)
    lax.fori_loop(0, ncs, fix, (one, zero, one, zero))

    y = z[:, :n_toep] + jnp.dot(h_ref[...].astype(BF16), c_ref[0], preferred_element_type=F32)
    o_ref[0] = _gelu_tanh(y).astype(o_ref.dtype)


def _ssm(u_g, w1, cmat, decay, ncs, nb):
    g_n, rows, n_toep = u_g.shape
    n_w = w1.shape[-1]
    n_st = n_w - n_toep
    return pl.pallas_call(
        functools.partial(_ssm_kernel, ncs=ncs, nb=nb, n_toep=n_toep),
        grid=(g_n,),
        in_specs=[pl.BlockSpec((1, rows, n_toep), lambda g: (g, 0, 0)),
                  pl.BlockSpec((1, n_toep, n_w), lambda g: (g, 0, 0)),
                  pl.BlockSpec((1, n_st, n_toep), lambda g: (g, 0, 0)),
                  pl.BlockSpec((1, 8, LANES), lambda g: (g, 0, 0))],
        out_specs=pl.BlockSpec((1, rows, n_toep), lambda g: (g, 0, 0)),
        out_shape=jax.ShapeDtypeStruct((g_n, rows, n_toep), BF16),
        scratch_shapes=[pltpu.VMEM((rows, n_st), F32), pltpu.VMEM((rows, n_st), F32)],
        compiler_params=_params(1),
        name="ssm",
    )(u_g, w1, cmat, decay)


def _key_features(lp, seq):
    pos = jnp.arange(lp, dtype=jnp.int32)
    live = (pos >= N_META).astype(F32)
    cols = [live, live,
            (pos >> ALIBI_SPLIT_BITS).astype(F32) * live,
            (pos & (ALIBI_SPLIT - 1)).astype(F32) * live,
            (pos >= seq).astype(F32)]
    return _pad_last(jnp.stack(cols, axis=1), LANES).astype(BF16)


def _attn_kernel(slope_ref, lam_ref, q_ref, k_ref, v_ref, kf_ref, g_ref, o_ref, *, tq, tk, lam_init):
    slope = slope_ref[pl.program_id(1)]
    lam = lam_ref[0]
    q0 = pl.program_id(2) * tq
    nk = k_ref.shape[0] // tk
    q = q_ref[...] * jnp.asarray(ATTN_HEAD_DIM ** -0.5, BF16)
    lane = lax.broadcasted_iota(jnp.int32, q.shape, 1)
    zero = jnp.zeros_like(q)
    q_maps = (jnp.where(lane < ATTN_HEAD_DIM, q, zero),
              jnp.where(lane >= ATTN_HEAD_DIM, q, zero))

    qpos = q0 + lax.broadcasted_iota(jnp.int32, q.shape, 0)
    live = jnp.where(qpos >= N_META, 1.0, 0.0)
    hi = lax.shift_right_logical(qpos, ALIBI_SPLIT_BITS).astype(F32) * (slope * ALIBI_SPLIT) * live
    lo = (qpos & (ALIBI_SPLIT - 1)).astype(F32) * slope * live
    dist_cols = jnp.where(lane == 0, -hi, jnp.where(lane == 1, -lo, jnp.where(
        lane == 2, slope * ALIBI_SPLIT * live, jnp.where(lane == 3, slope * live, 0.0))))
    pad_col = jnp.where(lane == 4, NEG_BIG, 0.0)
    f_left = (dist_cols + pad_col).astype(BF16)
    f_right = (pad_col - dist_cols).astype(BF16)

    nt = (((1,), (1,)), ((), ()))
    ones = jnp.ones((tk, LANES), BF16)

    def step(j, carry, feats):
        rows = pl.ds(pl.multiple_of(j * tk, tk), tk)
        kx = jnp.concatenate([k_ref[rows, :], kf_ref[rows, :]], axis=1)
        vx = jnp.concatenate([v_ref[rows, :], ones], axis=1)
        out = []
        for mp in (0, 1):
            m, acc = carry[2 * mp], carry[2 * mp + 1]
            s = None
            for f in feats:
                sf = lax.dot_general(jnp.concatenate([q_maps[mp], f], axis=1), kx, nt,
                                     preferred_element_type=F32)
                s = sf if s is None else jnp.minimum(s, sf)
            m_new = jnp.maximum(m, jnp.max(s, axis=-1, keepdims=True))
            p = jnp.exp(s - m_new).astype(BF16)
            acc = jnp.exp(m - m_new) * acc + jnp.dot(p, vx, preferred_element_type=F32)
            out += [m_new, acc]
        return tuple(out)

    diag = q0 // tk

    def off_diag(j, carry):
        jj = jnp.where(j >= diag, j + 1, j)
        return step(jj, carry, (jnp.where(jj < diag, f_left, f_right),))

    m0 = jnp.full((tq, 1), NEG_BIG, F32)
    a0 = jnp.zeros((tq, 2 * LANES), F32)
    carry = lax.fori_loop(0, nk - 1, off_diag, (m0, a0, m0, a0), unroll=ATTN_UNROLL)
    _, a1, _, a2 = step(diag, carry, (f_left, f_right))
    o = (a1[:, :ATTN_V_DIM] / a1[:, ATTN_V_DIM:ATTN_V_DIM + 1]
         - lam * (a2[:, :ATTN_V_DIM] / a2[:, ATTN_V_DIM:ATTN_V_DIM + 1]))
    o_ref[...] = (_rms(o, g_ref[...]) * (1.0 - lam_init)).astype(o_ref.dtype)


def _attention(proj, slopes, lam, subln_g, bsz, lp, seq, col0, lam_init):
    tp = proj.shape[0]
    tk, tq = ATTN_TK, ATTN_TQ
    assert lp % tk == 0 and tk % tq == 0
    nq = lp // tq
    qc, kc, vc = (col0 // LANES + i * ATTN_HEADS for i in range(3))
    fixed = lambda b, h, i: (0, 0)
    return pl.pallas_call(
        functools.partial(_attn_kernel, tq=tq, tk=tk, lam_init=lam_init),
        grid=(bsz, ATTN_HEADS, nq),
        in_specs=[pl.BlockSpec(memory_space=pltpu.SMEM),
                  pl.BlockSpec(memory_space=pltpu.SMEM),
                  pl.BlockSpec((tq, LANES), lambda b, h, i: (b * nq + i, qc + h)),
                  pl.BlockSpec((lp, LANES), lambda b, h, i: (b, kc + h)),
                  pl.BlockSpec((lp, LANES), lambda b, h, i: (b, vc + h)),
                  pl.BlockSpec((lp, LANES), fixed),
                  pl.BlockSpec((1, LANES), fixed)],
        out_specs=pl.BlockSpec((tq, LANES), lambda b, h, i: (b * nq + i, h)),
        out_shape=jax.ShapeDtypeStruct((tp, ATTN_HEADS * ATTN_V_DIM), BF16),
        compiler_params=_params(3),
        name="attn",
    )(slopes, lam, proj, proj, proj, _key_features(lp, seq), subln_g)


def _merge_kernel(y_ref, o_ref, gs_ref, ga_ref, wa_ref, wb_ref, wo_ref, out_ref):
    y = y_ref[...]
    glu_a = jnp.dot(y, wa_ref[...], preferred_element_type=F32)
    glu_b = jnp.dot(y, wb_ref[...], preferred_element_type=F32)
    y_attn = jnp.dot(o_ref[...], wo_ref[...], preferred_element_type=F32)
    y_ssm = glu_a * jax.nn.sigmoid(glu_b)
    merged = (jax.nn.sigmoid(gs_ref[...].astype(F32)) * y_ssm
              + jax.nn.sigmoid(ga_ref[...].astype(F32)) * y_attn)
    out_ref[...] = merged.astype(out_ref.dtype)


def _merge(y_gelu, o_attn, proj, w_glu, w_ao, gate_col0, d):
    tp, w = y_gelu.shape
    tm = _div_tile(tp, 640, LANES)
    tn = _div_tile(d, 512, LANES)
    nj = d // tn
    gs0 = gate_col0 // tn
    return pl.pallas_call(
        _merge_kernel,
        grid=(tp // tm, nj),
        in_specs=[pl.BlockSpec((tm, w), lambda i, j: (i, 0)),
                  pl.BlockSpec((tm, o_attn.shape[1]), lambda i, j: (i, 0)),
                  pl.BlockSpec((tm, tn), lambda i, j: (i, gs0 + j)),
                  pl.BlockSpec((tm, tn), lambda i, j: (i, gs0 + nj + j)),
                  pl.BlockSpec((w, tn), lambda i, j: (0, j)),
                  pl.BlockSpec((w, tn), lambda i, j: (0, nj + j)),
                  pl.BlockSpec((w_ao.shape[0], tn), lambda i, j: (0, j))],
        out_specs=pl.BlockSpec((tm, tn), lambda i, j: (i, j)),
        out_shape=jax.ShapeDtypeStruct((tp, d), BF16),
        compiler_params=_params(2),
        name="merge",
    )(y_gelu, o_attn, proj, proj, w_glu, w_glu, w_ao)


def _outproj_router_kernel(m_ref, h_ref, wo_ref, g_ref, wr_ref, br_ref,
                           h1_ref, hn_ref, idx_ref, gate_ref, cnt_ref, *, tm, lp, seq, bsz):
    i = pl.program_id(0)
    n_e = wr_ref.shape[1]

    @pl.when(i == 0)
    def _():
        cnt_ref[...] = jnp.zeros_like(cnt_ref)

    h1 = h_ref[...] + jnp.dot(m_ref[...], wo_ref[...], preferred_element_type=F32)
    h1_ref[...] = h1
    hn = _rms(h1, g_ref[...])
    hn_ref[...] = hn
    logits = jnp.dot(hn, wr_ref[...], preferred_element_type=F32, precision=HIGHEST) + br_ref[...]

    row = i * tm + lax.broadcasted_iota(jnp.int32, (tm, 1), 0)
    valid = row < 0
    for b in range(bsz):
        valid = valid | ((row >= b * lp) & (row < b * lp + seq))

    lane = lax.broadcasted_iota(jnp.int32, (tm, n_e), 1).astype(F32)
    work = logits
    vals, idxs, hots = [], [], []
    for _ in range(TOP_K):
        mk = jnp.max(work, axis=-1, keepdims=True)
        ik = jnp.min(jnp.where(work == mk, lane, float(n_e)), axis=-1, keepdims=True)
        hot = lane == ik
        work = jnp.where(hot, -jnp.inf, work)
        vals.append(mk)
        idxs.append(ik)
        hots.append(hot)
    exps = [jnp.exp(v - vals[0]) for v in vals]
    denom = exps[0]
    for e in exps[1:]:
        denom = denom + e

    multi = jnp.zeros((tm, n_e), F32)
    for hot in hots:
        multi = multi + jnp.where(hot & valid, 1.0, 0.0)
    r_i = lax.broadcasted_iota(jnp.int32, (tm, tm), 0)
    c_i = lax.broadcasted_iota(jnp.int32, (tm, tm), 1)
    tri = jnp.where(r_i > c_i, 1.0, 0.0).astype(BF16)
    before = jnp.dot(tri, multi.astype(BF16), preferred_element_type=F32) + cnt_ref[0:1, 0:n_e]

    lane_o = lax.broadcasted_iota(jnp.int32, (tm, LANES), 1)
    idx_out = jnp.zeros((tm, LANES), jnp.int32)
    gate_out = jnp.zeros((tm, LANES), F32)
    for k in range(TOP_K):
        rank = jnp.sum(jnp.where(hots[k], before, 0.0), axis=-1, keepdims=True).astype(jnp.int32)
        idx_out = jnp.where(lane_o == k, idxs[k].astype(jnp.int32), idx_out)
        idx_out = jnp.where(lane_o == TOP_K + k, rank, idx_out)
        gate_out = jnp.where(lane_o == k, exps[k] / denom, gate_out)
    idx_ref[...] = idx_out
    gate_ref[...] = gate_out
    cnt_ref[0:1, 0:n_e] = cnt_ref[0:1, 0:n_e] + jnp.sum(multi, axis=0, keepdims=True)


def _outproj_router(merged, h, w_out, g, w_router, b_router, lp, seq, bsz):
    tp, d = h.shape
    n_e = w_router.shape[1]
    tm = _div_tile(tp, 256, LANES)
    row = lambda i: (i, 0)
    fixed = lambda i: (0, 0)
    return pl.pallas_call(
        functools.partial(_outproj_router_kernel, tm=tm, lp=lp, seq=seq, bsz=bsz),
        grid=(tp // tm,),
        in_specs=[pl.BlockSpec((tm, d), row), pl.BlockSpec((tm, d), row),
                  pl.BlockSpec((d, d), fixed), pl.BlockSpec((1, d), fixed),
                  pl.BlockSpec((d, n_e), fixed), pl.BlockSpec((1, n_e), fixed)],
        out_specs=[pl.BlockSpec((tm, d), row), pl.BlockSpec((tm, d), row),
                   pl.BlockSpec((tm, LANES), row), pl.BlockSpec((tm, LANES), row),
                   pl.BlockSpec((8, LANES), fixed)],
        out_shape=[jax.ShapeDtypeStruct((tp, d), F32), jax.ShapeDtypeStruct((tp, d), F32),
                   jax.ShapeDtypeStruct((tp, LANES), jnp.int32), jax.ShapeDtypeStruct((tp, LANES), F32),
                   jax.ShapeDtypeStruct((8, LANES), F32)],
        compiler_params=_params(1),
        name="outproj_router",
    )(merged, h, w_out, g, w_router, b_router)


def _row_copy(src_hbm, row, dst, dst_row, sem):
    return pltpu.make_async_copy(src_hbm.at[pl.ds(row, 1), :], dst.at[pl.ds(dst_row, 1), :], sem)


def _dispatch_kernel(slot_ref, hn_ref, xs_in, xs_hbm, sem, *, tm):
    del xs_in
    i = pl.program_id(0)

    def issue(r, c):
        base = (i * tm + r) * TOP_K
        for k in range(TOP_K):
            pltpu.make_async_copy(hn_ref.at[pl.ds(r, 1), :],
                                  xs_hbm.at[pl.ds(slot_ref[base + k], 1), :], sem).start()
        return c

    lax.fori_loop(0, tm, issue, 0, unroll=DMA_UNROLL)
    for _ in range(TOP_K):
        pltpu.make_async_copy(hn_ref, xs_hbm.at[pl.ds(0, tm), :], sem).wait()


def _dispatch(slot_flat, hn, n_rows):
    tp, d = hn.shape
    tm = _div_tile(tp, 256, LANES)
    return pl.pallas_call(
        functools.partial(_dispatch_kernel, tm=tm),
        grid_spec=pltpu.PrefetchScalarGridSpec(
            num_scalar_prefetch=1,
            grid=(tp // tm,),
            in_specs=[pl.BlockSpec((tm, d), lambda i, s: (i, 0)),
                      pl.BlockSpec(memory_space=pl.ANY)],
            out_specs=pl.BlockSpec(memory_space=pl.ANY),
            scratch_shapes=[pltpu.SemaphoreType.DMA]),
        out_shape=jax.ShapeDtypeStruct((n_rows, d), hn.dtype),
        input_output_aliases={2: 0},
        compiler_params=_params(1),
        name="dispatch",
    )(slot_flat, hn, jnp.zeros((n_rows, d), hn.dtype))


def _w1_prep_kernel(w_ref, o_ref):
    r = lax.broadcasted_iota(jnp.int32, (MXU_DIM, MXU_DIM), 0)
    c = lax.broadcasted_iota(jnp.int32, (MXU_DIM, MXU_DIM), 1)
    sel = jnp.where(r == jnp.where(c < LANES, 2 * c, 2 * (c - LANES) + 1), 1.0, 0.0).astype(BF16)
    for g in range(w_ref.shape[2] // MXU_DIM):
        cols = slice(g * MXU_DIM, (g + 1) * MXU_DIM)
        o_ref[0, :, cols] = jnp.dot(w_ref[0, :, cols].astype(BF16), sel,
                                    preferred_element_type=F32).astype(BF16)


def _w1_prep(w1):
    n_e, d, f2 = w1.shape
    tw = _div_tile(f2, 1024, MXU_DIM)
    spec = pl.BlockSpec((1, d, tw), lambda e, j: (e, 0, j))
    return pl.pallas_call(
        _w1_prep_kernel,
        grid=(n_e, f2 // tw),
        in_specs=[spec],
        out_specs=spec,
        out_shape=jax.ShapeDtypeStruct(w1.shape, BF16),
        compiler_params=_params(2),
        name="w1_prep",
    )(w1)


def _ffn1_kernel(be_ref, bm_ref, nused_ref, x_ref, w_ref, b_ref, o_ref):
    @pl.when(pl.program_id(1) < nused_ref[0])
    def _():
        h = jnp.dot(x_ref[...].astype(BF16), w_ref[0], preferred_element_type=F32) + b_ref[0]
        for g in range(h.shape[1] // MXU_DIM):
            glu = jnp.minimum(h[:, g * MXU_DIM:g * MXU_DIM + LANES], SWIGLU_LIMIT)
            lin = jnp.clip(h[:, g * MXU_DIM + LANES:(g + 1) * MXU_DIM], -SWIGLU_LIMIT, SWIGLU_LIMIT)
            o_ref[:, g * LANES:(g + 1) * LANES] = (
                glu * jax.nn.sigmoid(SWIGLU_ALPHA * glu) * (lin + 1.0)).astype(o_ref.dtype)

    @pl.when(pl.program_id(1) >= nused_ref[0])
    def _():
        o_ref[...] = jnp.zeros_like(o_ref)


def _ffn1(blk_expert, blk_map, n_used, xs, w1g, b1g, pb):
    n_slots = blk_map.shape[0] * pb
    d = xs.shape[1]
    f = w1g.shape[-1] // 2
    tn = _div_tile(f, 1024, LANES)
    return pl.pallas_call(
        _ffn1_kernel,
        grid_spec=pltpu.PrefetchScalarGridSpec(
            num_scalar_prefetch=3,
            grid=(f // tn, n_slots // pb),
            in_specs=[pl.BlockSpec((pb, d), lambda j, i, be, bm, nu: (bm[i], 0)),
                      pl.BlockSpec((1, d, 2 * tn), lambda j, i, be, bm, nu: (be[i], 0, j)),
                      pl.BlockSpec((1, 1, 2 * tn), lambda j, i, be, bm, nu: (be[i], 0, j))],
            out_specs=pl.BlockSpec((pb, tn), lambda j, i, be, bm, nu: (i, j))),
        out_shape=jax.ShapeDtypeStruct((n_slots, f), BF16),
        compiler_params=_params(2),
        name="ffn1",
    )(blk_expert, blk_map, n_used, xs, w1g, b1g)


def _ffn2_kernel(be_ref, bm_ref, nused_ref, a_ref, w_ref, b_ref, o_ref, wb_ref):
    i = pl.program_id(1)

    @pl.when((i == 0) | (be_ref[i] != be_ref[jnp.maximum(i - 1, 0)]))
    def _():
        wb_ref[...] = w_ref[0].astype(BF16)

    @pl.when(i < nused_ref[0])
    def _():
        o_ref[...] = jnp.dot(a_ref[...], wb_ref[...], preferred_element_type=F32) + b_ref[0]

    @pl.when(i >= nused_ref[0])
    def _():
        o_ref[...] = jnp.zeros_like(o_ref)


def _ffn2(blk_expert, blk_map, n_used, act, w2, b2, pb):
    n_slots, f = act.shape
    d = w2.shape[-1]
    tn = _div_tile(d, 1024, LANES)
    return pl.pallas_call(
        _ffn2_kernel,
        grid_spec=pltpu.PrefetchScalarGridSpec(
            num_scalar_prefetch=3,
            grid=(d // tn, n_slots // pb),
            in_specs=[pl.BlockSpec((pb, f), lambda j, i, be, bm, nu: (bm[i], 0)),
                      pl.BlockSpec((1, f, tn), lambda j, i, be, bm, nu: (be[i], 0, j)),
                      pl.BlockSpec((1, 1, tn), lambda j, i, be, bm, nu: (be[i], 0, j))],
            out_specs=pl.BlockSpec((pb, tn), lambda j, i, be, bm, nu: (i, j)),
            scratch_shapes=[pltpu.VMEM((f, tn), BF16)]),
        out_shape=jax.ShapeDtypeStruct((n_slots, d), F32),
        compiler_params=_params(2),
        name="ffn2",
    )(blk_expert, blk_map, n_used, act, w2, b2)


def _combine_kernel(slot_ref, ys_hbm, h1_ref, gate_ref, g_ref, o_ref, buf, sem, *, tm):
    i = pl.program_id(0)

    def issue(r, c):
        for k in range(TOP_K):
            _row_copy(ys_hbm, slot_ref[(i * tm + r) * TOP_K + k], buf.at[k], r, sem).start()
        return c

    lax.fori_loop(0, tm, issue, 0, unroll=DMA_UNROLL)
    for k in range(TOP_K):
        pltpu.make_async_copy(ys_hbm.at[pl.ds(0, tm), :], buf.at[k], sem).wait()
    gates = gate_ref[...]
    h2 = h1_ref[...]
    for k in range(TOP_K):
        h2 = h2 + gates[:, k:k + 1] * buf[k]
    o_ref[...] = _rms(h2, g_ref[...])


def _combine(slot_flat, ys, h1, gates, g):
    tp, d = h1.shape
    tm = _div_tile(tp, 128, LANES)
    row = lambda i, s: (i, 0)
    return pl.pallas_call(
        functools.partial(_combine_kernel, tm=tm),
        grid_spec=pltpu.PrefetchScalarGridSpec(
            num_scalar_prefetch=1,
            grid=(tp // tm,),
            in_specs=[pl.BlockSpec(memory_space=pl.ANY),
                      pl.BlockSpec((tm, d), row), pl.BlockSpec((tm, LANES), row),
                      pl.BlockSpec((1, d), lambda i, s: (0, 0))],
            out_specs=pl.BlockSpec((tm, d), row),
            scratch_shapes=[pltpu.VMEM((TOP_K, tm, d), F32), pltpu.SemaphoreType.DMA]),
        out_shape=jax.ShapeDtypeStruct((tp, d), F32),
        compiler_params=_params(1),
        name="combine",
    )(slot_flat, ys, h1, gates, g)


def kernel(x, meta_tokens, norm_mix_g, w_in, ssm_lam_re_f, ssm_lam_im_f, ssm_log_dt_f, ssm_b_re_f, ssm_b_im_f, ssm_c_re_f, ssm_c_im_f, ssm_lam_re_b, ssm_lam_im_b, ssm_log_dt_b, ssm_b_re_b, ssm_b_im_b, ssm_c_re_b, ssm_c_im_b, ssm_d, w_ssm_glu, diff_lam_q1, diff_lam_k1, diff_lam_q2, diff_lam_k2, diff_subln_g, w_attn_out, w_out, norm_moe_g, w_router, b_router, w_exp1, b_exp1, w_exp2, b_exp2, norm_final_g):
    bsz, s_len, d = x.shape
    depth = w_in.shape[0]
    assert depth == 1, "the combine kernel fuses the final RMSNorm, so exactly one layer is supported"
    seq = s_len + N_META
    lp = -(-seq // ATTN_TK) * ATTN_TK
    tp = bsz * lp
    ssm_w = ssm_d.shape[-1]
    g_n = ssm_w // SSM_GROUP
    qk_w = ATTN_HEADS * 2 * ATTN_HEAD_DIM
    v_w = ATTN_HEADS * ATTN_V_DIM
    n_e = w_router.shape[-1]
    f_ff = w_exp2.shape[2]
    tc = SSM_CHUNK
    nc = lp // tc
    assert 8 % bsz == 0, "the S5 scan packs batch x sequence segments into 8-row tiles"
    nseg = 8 // bsz
    assert nc % nseg == 0
    ncs = nc // nseg
    pb = MOE_BLOCK

    meta = jnp.broadcast_to(meta_tokens.astype(x.dtype)[None], (bsz, N_META, d))
    h = jnp.concatenate([meta, x, jnp.zeros((bsz, lp - seq, d), x.dtype)], axis=1).reshape(tp, d)

    pos = jnp.arange(tp, dtype=jnp.int32) % lp
    tok_valid = pos < seq
    slopes = jnp.exp2(-ALIBI_MAX_EXP * jnp.arange(1, ATTN_HEADS + 1, dtype=F32) / ATTN_HEADS)

    for layer in range(depth):
        proj = _inproj(h, norm_mix_g[layer][None], w_in[layer].astype(BF16))

        u_g = proj[:, :ssm_w].reshape(bsz, nseg, ncs, tc, g_n, SSM_GROUP)
        u_g = u_g.transpose(4, 2, 1, 0, 3, 5).reshape(g_n, nc * bsz, tc * SSM_GROUP)
        fw = (ssm_lam_re_f, ssm_lam_im_f, ssm_log_dt_f, ssm_b_re_f, ssm_b_im_f, ssm_c_re_f, ssm_c_im_f)
        bw = (ssm_lam_re_b, ssm_lam_im_b, ssm_log_dt_b, ssm_b_re_b, ssm_b_im_b, ssm_c_re_b, ssm_c_im_b)
        w1, cmat, decay = _ssm_operators([p[layer].astype(F32) for p in fw],
                                         [p[layer].astype(F32) for p in bw],
                                         ssm_d[layer].astype(F32), tc, ncs)
        y_g = _ssm(u_g, w1, cmat, decay, ncs, bsz)
        y_gelu = y_g.reshape(g_n, ncs, nseg, bsz, tc, SSM_GROUP).transpose(3, 2, 1, 4, 0, 5).reshape(tp, ssm_w)

        lam_init = 0.8 - 0.6 * math.exp(-0.3 * layer)
        lam = (jnp.exp(jnp.sum(diff_lam_q1[layer].astype(F32) * diff_lam_k1[layer].astype(F32)))
               - jnp.exp(jnp.sum(diff_lam_q2[layer].astype(F32) * diff_lam_k2[layer].astype(F32)))
               + lam_init)
        o_attn = _attention(proj, slopes, lam.reshape(1), diff_subln_g[layer][None].astype(F32),
                            bsz, lp, seq, ssm_w, lam_init)

        merged = _merge(y_gelu, o_attn, proj, w_ssm_glu[layer].astype(BF16),
                        w_attn_out[layer].astype(BF16), ssm_w + 2 * qk_w + v_w, d)

        h1, hn, ridx, rgate, rcnt = _outproj_router(
            merged, h, w_out[layer].astype(BF16), norm_moe_g[layer][None].astype(F32),
            w_router[layer].astype(F32), b_router[layer][None].astype(F32), lp, seq, bsz)

        e_idx = ridx[:, :TOP_K]
        rank = ridx[:, TOP_K:2 * TOP_K]
        counts = rcnt[0, :n_e].astype(jnp.int32)
        padded = ((counts + pb - 1) // pb) * pb
        pad_end = jnp.cumsum(padded)
        pad_start = pad_end - padded
        n_blocks = -(-(bsz * seq * TOP_K) // pb) + n_e
        n_slots = n_blocks * pb
        slot = pad_start[e_idx] + rank
        n_used = (pad_end[-1] // pb).astype(jnp.int32)
        blk_map = jnp.minimum(jnp.arange(n_blocks, dtype=jnp.int32), jnp.maximum(n_used - 1, 0))
        blk_expert = jnp.minimum(jnp.sum(pad_end[None, :] <= (blk_map * pb)[:, None], axis=1), n_e - 1).astype(jnp.int32)
        n_used1 = n_used.reshape(1)

        w1g = _w1_prep(w_exp1[layer].astype(F32))
        b1g = (b_exp1[layer].astype(F32).reshape(n_e, 2 * f_ff // MXU_DIM, LANES, 2)
               .transpose(0, 1, 3, 2).reshape(n_e, 1, 2 * f_ff))

        n_scratch = bsz * (lp - seq) * TOP_K
        scratch = n_slots + (jnp.cumsum(~tok_valid) - 1)[:, None] * TOP_K + jnp.arange(TOP_K)[None, :]
        xs = _dispatch(jnp.where(tok_valid[:, None], slot, scratch).astype(jnp.int32).reshape(-1),
                       hn, n_slots + n_scratch)
        act = _ffn1(blk_expert, blk_map, n_used1, xs, w1g, b1g, pb)
        ys = _ffn2(blk_expert, blk_map, n_used1, act, w_exp2[layer].astype(F32),
                   b_exp2[layer].astype(F32)[:, None, :], pb)

        slot_flat = jnp.where(tok_valid[:, None], slot, 0).reshape(-1)
        h = _combine(slot_flat, ys, h1, rgate, norm_final_g[None].astype(F32))
    return h.reshape(bsz, lp, d)[:, N_META:seq]
```

```python
import functools
import math

import jax
import jax.numpy as jnp
from jax import lax
from jax.experimental import pallas as pl
from jax.experimental.pallas import tpu as pltpu

F32 = jnp.float32
BF16 = jnp.bfloat16

N_META = 16
NORM_EPS = 1e-5
SSM_GROUP = 16
SSM_STATE = 64
ATTN_HEADS = 8
ATTN_HEAD_DIM = 64
ATTN_V_DIM = 2 * ATTN_HEAD_DIM
ALIBI_MAX_EXP = 8.0
ALIBI_SPLIT_BITS = 7
ALIBI_SPLIT = 1 << ALIBI_SPLIT_BITS
TOP_K = 4
SWIGLU_LIMIT = 7.0
SWIGLU_ALPHA = 1.702

LANES = 128
MXU_DIM = 256
ATTN_TK = 3 * MXU_DIM
ATTN_TQ = ATTN_TK
ATTN_UNROLL = 5
DMA_UNROLL = 8
SSM_CHUNK = 32
MOE_BLOCK = 512
VMEM_LIMIT = 56 * 1024 * 1024
NEG_BIG = -1e30
HIGHEST = lax.Precision.HIGHEST


def _div_tile(n, target, mult):
    best = None
    for t in range(mult, min(n, target) + 1, mult):
        if n % t == 0:
            best = t
    assert best is not None, (n, target, mult)
    return best


def _params(n_axes):
    return pltpu.CompilerParams(dimension_semantics=("arbitrary",) * n_axes,
                                vmem_limit_bytes=VMEM_LIMIT)


def _rms(x, g):
    return x * lax.rsqrt(jnp.mean(x * x, axis=-1, keepdims=True) + NORM_EPS) * g


def _inproj_kernel(h_ref, g_ref, w_ref, o_ref, xn_ref):
    @pl.when(pl.program_id(1) == 0)
    def _():
        xn_ref[...] = _rms(h_ref[...], g_ref[...]).astype(BF16)

    o_ref[...] = jnp.dot(xn_ref[...], w_ref[...], preferred_element_type=F32).astype(o_ref.dtype)


def _inproj(h, g, w_bf16):
    tp, d = h.shape
    n = w_bf16.shape[1]
    tm = _div_tile(tp, 768, LANES)
    tn = _div_tile(n, 1024, LANES)
    return pl.pallas_call(
        _inproj_kernel,
        grid=(tp // tm, n // tn),
        in_specs=[pl.BlockSpec((tm, d), lambda i, j: (i, 0)),
                  pl.BlockSpec((1, d), lambda i, j: (0, 0)),
                  pl.BlockSpec((d, tn), lambda i, j: (0, j))],
        out_specs=pl.BlockSpec((tm, tn), lambda i, j: (i, j)),
        out_shape=jax.ShapeDtypeStruct((tp, n), BF16),
        scratch_shapes=[pltpu.VMEM((tm, d), BF16)],
        compiler_params=_params(2),
        name="inproj",
    )(h, g, w_bf16)


def _ssm_direction_terms(lam_re, lam_im, log_dt, b_re, b_im):
    dt = jnp.exp(log_dt)[:, None]
    zr = lam_re * dt
    zi = lam_im * dt

    def apow(n):
        n = n.astype(F32)[:, None, None]
        mag = jnp.exp(n * zr[None])
        return mag * jnp.cos(n * zi[None]), mag * jnp.sin(n * zi[None])

    ar, ai = apow(jnp.ones((1,), F32))
    ar, ai = ar[0], ai[0]
    den = lam_re * lam_re + lam_im * lam_im
    fr = ((ar - 1.0) * lam_re + ai * lam_im) / den
    fi = (ai * lam_re - (ar - 1.0) * lam_im) / den
    bbr = fr[..., None] * b_re - fi[..., None] * b_im
    bbi = fr[..., None] * b_im + fi[..., None] * b_re
    return apow, bbr, bbi


def _pad_last(x, n):
    return jnp.pad(x, [(0, 0)] * (x.ndim - 1) + [(0, n - x.shape[-1])])


def _ssm_operators(fw, bw, ssm_d, tc, ncs):
    g_n, p_n = fw[0].shape
    io = SSM_GROUP
    n = jnp.arange(tc)
    terms = []
    for (lam_re, lam_im, log_dt, b_re, b_im, c_re, c_im), rev in ((fw, False), (bw, True)):
        apow, bbr, bbi = _ssm_direction_terms(lam_re, lam_im, log_dt, b_re, b_im)
        er, ei = apow(n)
        mr = c_re[None] * er[:, :, None, :] - c_im[None] * ei[:, :, None, :]
        mi = c_re[None] * ei[:, :, None, :] + c_im[None] * er[:, :, None, :]
        kern = (jnp.einsum('ngop,gpi->ngoi', mr, bbr, precision=HIGHEST)
                - jnp.einsum('ngop,gpi->ngoi', mi, bbi, precision=HIGHEST))
        es_r, es_i = (er, ei) if rev else (er[::-1], ei[::-1])
        st_re = es_r[..., None] * bbr[None] - es_i[..., None] * bbi[None]
        st_im = es_r[..., None] * bbi[None] + es_i[..., None] * bbr[None]
        st = [_pad_last(x.transpose(1, 0, 3, 2).reshape(g_n, tc * io, p_n), LANES)
              for x in (st_re, st_im)]
        cr_, ci_ = apow((tc - n) if rev else (n + 1))
        rr = c_re[None] * cr_[:, :, None, :] - c_im[None] * ci_[:, :, None, :]
        ri = c_re[None] * ci_[:, :, None, :] + c_im[None] * cr_[:, :, None, :]
        ro = [jnp.pad(x.transpose(1, 3, 0, 2).reshape(g_n, p_n, tc * io),
                      ((0, 0), (0, LANES - p_n), (0, 0))) for x in (rr, -ri)]
        dr, di = apow(jnp.array([tc, tc * ncs]))
        dec = [_pad_last(dr, LANES), _pad_last(di, LANES)]
        terms.append((kern, st, ro, dec))
    (kf, stf, rof, decf), (kb, stb, rob, decb) = terms
    dmat = jnp.eye(io, dtype=F32)[None] * ssm_d.reshape(g_n, io)[:, :, None]
    kcat = jnp.concatenate([kb[:0:-1], kf[:1] + kb[:1] + dmat[None], kf[1:]], axis=0)
    idx = (n[None, :] - n[:, None]) + (tc - 1)
    toep = kcat[idx]
    toep = toep.transpose(2, 0, 4, 1, 3).reshape(g_n, tc * io, tc * io)
    w1 = jnp.concatenate([toep] + stf + stb, axis=-1).astype(BF16)
    cmat = jnp.concatenate(rof + rob, axis=1).astype(BF16)
    chunk_seg = [jnp.stack([decf[0][k], decf[1][k], decb[0][k], decb[1][k]], axis=1) for k in (0, 1)]
    decay = jnp.concatenate(chunk_seg, axis=1)
    return w1, cmat, decay


def _gelu_tanh(y):
    return 0.5 * y * (1.0 + jnp.tanh(math.sqrt(2.0 / math.pi) * (y + 0.044715 * (y * y * y))))


def _cmul(ar, ai, xr, xi):
    return ar * xr - ai * xi, ar * xi + ai * xr


def _ssm_kernel(u_ref, w_ref, c_ref, dec_ref, o_ref, s_ref, h_ref, *, ncs, nb, n_toep):
    z = jnp.dot(u_ref[0], w_ref[0], preferred_element_type=F32)
    s_ref[...] = z[:, n_toep:]
    dec = dec_ref[0]
    afr, afi, abr, abi = dec[0:1], dec[1:2], dec[2:3], dec[3:4]
    sfr, sfi, sbr, sbi = dec[4:5], dec[5:6], dec[6:7], dec[7:8]
    lf_re, lf_im = slice(0, LANES), slice(LANES, 2 * LANES)
    lb_re, lb_im = slice(2 * LANES, 3 * LANES), slice(3 * LANES, 4 * LANES)

    def tiles(c):
        return (pl.ds(pl.multiple_of(c * 8, 8), 8), pl.ds(pl.multiple_of((ncs - 1 - c) * 8, 8), 8))

    def local(c, carry):
        hfr, hfi, hbr, hbi = carry
        rf, rb = tiles(c)
        h_ref[rf, lf_re] = hfr
        h_ref[rf, lf_im] = hfi
        h_ref[rb, lb_re] = hbr
        h_ref[rb, lb_im] = hbi
        nfr, nfi = _cmul(afr, afi, hfr, hfi)
        nbr, nbi = _cmul(abr, abi, hbr, hbi)
        return (nfr + s_ref[rf, lf_re], nfi + s_ref[rf, lf_im],
                nbr + s_ref[rb, lb_re], nbi + s_ref[rb, lb_im])

    zero = jnp.zeros((8, LANES), F32)
    efr, efi, ebr, ebi = lax.fori_loop(0, ncs, local, (zero, zero, zero, zero))

    row8 = lax.broadcasted_iota(jnp.int32, (8, LANES), 0)

    def shift_dn(x):
        return jnp.where(row8 >= nb, pltpu.roll(x, nb, 0), 0.0)

    def shift_up(x):
        return jnp.where(row8 < 8 - nb, pltpu.roll(x, 8 - nb, 0), 0.0)

    def segment_in(er, ei, ar, ai, shift):
        xr, xi = shift(er), shift(ei)
        ir, ii = xr, xi
        for _ in range(8 // nb - 2):
            tr, ti = _cmul(ar, ai, shift(ir), shift(ii))
            ir, ii = xr + tr, xi + ti
        return ir, ii

    ifr, ifi = segment_in(efr, efi, sfr, sfi, shift_dn)
    ibr, ibi = segment_in(ebr, ebi, sbr, sbi, shift_up)

    def fix(c, carry):
        pfr, pfi, pbr, pbi = carry
        rf, rb = tiles(c)
        dfr, dfi = _cmul(pfr, pfi, ifr, ifi)
        dbr, dbi = _cmul(pbr, pbi, ibr, ibi)
        h_ref[rf, lf_re] = h_ref[rf, lf_re] + dfr
        h_ref[rf, lf_im] = h_ref[rf, lf_im] + dfi
        h_ref[rb, lb_re] = h_ref[rb, lb_re] + dbr
        h_ref[rb, lb_im] = h_ref[rb, lb_im] + dbi
        return _cmul(afr, afi, pfr, pfi) + _cmul(abr, abi, pbr, pbi)

    one = jnp.ones((8, LANES), F32)
    lax.fori_loop(0, ncs, fix, (one, zero, one, zero))

    y = z[:, :n_toep] + jnp.dot(h_ref[...].astype(BF16), c_ref[0], preferred_element_type=F32)
    o_ref[0] = _gelu_tanh(y).astype(o_ref.dtype)


def _ssm(u_g, w1, cmat, decay, ncs, nb):
    g_n, rows, n_toep = u_g.shape
    n_w = w1.shape[-1]
    n_st = n_w - n_toep
    return pl.pallas_call(
        functools.partial(_ssm_kernel, ncs=ncs, nb=nb, n_toep=n_toep),
        grid=(g_n,),
        in_specs=[pl.BlockSpec((1, rows, n_toep), lambda g: (g, 0, 0)),
                  pl.BlockSpec((1, n_toep, n_w), lambda g: (g, 0, 0)),
                  pl.BlockSpec((1, n_st, n_toep), lambda g: (g, 0, 0)),
                  pl.BlockSpec((1, 8, LANES), lambda g: (g, 0, 0))],
        out_specs=pl.BlockSpec((1, rows, n_toep), lambda g: (g, 0, 0)),
        out_shape=jax.ShapeDtypeStruct((g_n, rows, n_toep), BF16),
        scratch_shapes=[pltpu.VMEM((rows, n_st), F32), pltpu.VMEM((rows, n_st), F32)],
        compiler_params=_params(1),
        name="ssm",
    )(u_g, w1, cmat, decay)


def _key_features(lp, seq):
    pos = jnp.arange(lp, dtype=jnp.int32)
    live = (pos >= N_META).astype(F32)
    cols = [live, live,
            (pos >> ALIBI_SPLIT_BITS).astype(F32) * live,
            (pos & (ALIBI_SPLIT - 1)).astype(F32) * live,
            (pos >= seq).astype(F32)]
    return _pad_last(jnp.stack(cols, axis=1), LANES).astype(BF16)


def _attn_kernel(slope_ref, lam_ref, q_ref, k_ref, v_ref, kf_ref, g_ref, o_ref, *, tq, tk, lam_init):
    slope = slope_ref[pl.program_id(1)]
    lam = lam_ref[0]
    q0 = pl.program_id(2) * tq
    nk = k_ref.shape[0] // tk
    q = q_ref[...] * jnp.asarray(ATTN_HEAD_DIM ** -0.5, BF16)
    lane = lax.broadcasted_iota(jnp.int32, q.shape, 1)
    zero = jnp.zeros_like(q)
    q_maps = (jnp.where(lane < ATTN_HEAD_DIM, q, zero),
              jnp.where(lane >= ATTN_HEAD_DIM, q, zero))

    qpos = q0 + lax.broadcasted_iota(jnp.int32, q.shape, 0)
    live = jnp.where(qpos >= N_META, 1.0, 0.0)
    hi = lax.shift_right_logical(qpos, ALIBI_SPLIT_BITS).astype(F32) * (slope * ALIBI_SPLIT) * live
    lo = (qpos & (ALIBI_SPLIT - 1)).astype(F32) * slope * live
    dist_cols = jnp.where(lane == 0, -hi, jnp.where(lane == 1, -lo, jnp.where(
        lane == 2, slope * ALIBI_SPLIT * live, jnp.where(lane == 3, slope * live, 0.0))))
    pad_col = jnp.where(lane == 4, NEG_BIG, 0.0)
    f_left = (dist_cols + pad_col).astype(BF16)
    f_right = (pad_col - dist_cols).astype(BF16)

    nt = (((1,), (1,)), ((), ()))
    ones = jnp.ones((tk, LANES), BF16)

    def step(j, carry, feats):
        rows = pl.ds(pl.multiple_of(j * tk, tk), tk)
        kx = jnp.concatenate([k_ref[rows, :], kf_ref[rows, :]], axis=1)
        vx = jnp.concatenate([v_ref[rows, :], ones], axis=1)
        out = []
        for mp in (0, 1):
            m, acc = carry[2 * mp], carry[2 * mp + 1]
            s = None
            for f in feats:
                sf = lax.dot_general(jnp.concatenate([q_maps[mp], f], axis=1), kx, nt,
                                     preferred_element_type=F32)
                s = sf if s is None else jnp.minimum(s, sf)
            m_new = jnp.maximum(m, jnp.max(s, axis=-1, keepdims=True))
            p = jnp.exp(s - m_new).astype(BF16)
            acc = jnp.exp(m - m_new) * acc + jnp.dot(p, vx, preferred_element_type=F32)
            out += [m_new, acc]
        return tuple(out)

    diag = q0 // tk

    def off_diag(j, carry):
        jj = jnp.where(j >= diag, j + 1, j)
        return step(jj, carry, (jnp.where(jj < diag, f_left, f_right),))

    m0 = jnp.full((tq, 1), NEG_BIG, F32)
    a0 = jnp.zeros((tq, 2 * LANES), F32)
    carry = lax.fori_loop(0, nk - 1, off_diag, (m0, a0, m0, a0), unroll=ATTN_UNROLL)
    _, a1, _, a2 = step(diag, carry, (f_left, f_right))
    o = (a1[:, :ATTN_V_DIM] / a1[:, ATTN_V_DIM:ATTN_V_DIM + 1]
         - lam * (a2[:, :ATTN_V_DIM] / a2[:, ATTN_V_DIM:ATTN_V_DIM + 1]))
    o_ref[...] = (_rms(o, g_ref[...]) * (1.0 - lam_init)).astype(o_ref.dtype)


def _attention(proj, slopes, lam, subln_g, bsz, lp, seq, col0, lam_init):
    tp = proj.shape[0]
    tk, tq = ATTN_TK, ATTN_TQ
    assert lp % tk == 0 and tk % tq == 0
    nq = lp // tq
    qc, kc, vc = (col0 // LANES + i * ATTN_HEADS for i in range(3))
    fixed = lambda b, h, i: (0, 0)
    return pl.pallas_call(
        functools.partial(_attn_kernel, tq=tq, tk=tk, lam_init=lam_init),
        grid=(bsz, ATTN_HEADS, nq),
        in_specs=[pl.BlockSpec(memory_space=pltpu.SMEM),
                  pl.BlockSpec(memory_space=pltpu.SMEM),
                  pl.BlockSpec((tq, LANES), lambda b, h, i: (b * nq + i, qc + h)),
                  pl.BlockSpec((lp, LANES), lambda b, h, i: (b, kc + h)),
                  pl.BlockSpec((lp, LANES), lambda b, h, i: (b, vc + h)),
                  pl.BlockSpec((lp, LANES), fixed),
                  pl.BlockSpec((1, LANES), fixed)],
        out_specs=pl.BlockSpec((tq, LANES), lambda b, h, i: (b * nq + i, h)),
        out_shape=jax.ShapeDtypeStruct((tp, ATTN_HEADS * ATTN_V_DIM), BF16),
        compiler_params=_params(3),
        name="attn",
    )(slopes, lam, proj, proj, proj, _key_features(lp, seq), subln_g)


def _merge_kernel(y_ref, o_ref, gs_ref, ga_ref, wa_ref, wb_ref, wo_ref, out_ref):
    y = y_ref[...]
    glu_a = jnp.dot(y, wa_ref[...], preferred_element_type=F32)
    glu_b = jnp.dot(y, wb_ref[...], preferred_element_type=F32)
    y_attn = jnp.dot(o_ref[...], wo_ref[...], preferred_element_type=F32)
    y_ssm = glu_a * jax.nn.sigmoid(glu_b)
    merged = (jax.nn.sigmoid(gs_ref[...].astype(F32)) * y_ssm
              + jax.nn.sigmoid(ga_ref[...].astype(F32)) * y_attn)
    out_ref[...] = merged.astype(out_ref.dtype)


def _merge(y_gelu, o_attn, proj, w_glu, w_ao, gate_col0, d):
    tp, w = y_gelu.shape
    tm = _div_tile(tp, 640, LANES)
    tn = _div_tile(d, 512, LANES)
    nj = d // tn
    gs0 = gate_col0 // tn
    return pl.pallas_call(
        _merge_kernel,
        grid=(tp // tm, nj),
        in_specs=[pl.BlockSpec((tm, w), lambda i, j: (i, 0)),
                  pl.BlockSpec((tm, o_attn.shape[1]), lambda i, j: (i, 0)),
                  pl.BlockSpec((tm, tn), lambda i, j: (i, gs0 + j)),
                  pl.BlockSpec((tm, tn), lambda i, j: (i, gs0 + nj + j)),
                  pl.BlockSpec((w, tn), lambda i, j: (0, j)),
                  pl.BlockSpec((w, tn), lambda i, j: (0, nj + j)),
                  pl.BlockSpec((w_ao.shape[0], tn), lambda i, j: (0, j))],
        out_specs=pl.BlockSpec((tm, tn), lambda i, j: (i, j)),
        out_shape=jax.ShapeDtypeStruct((tp, d), BF16),
        compiler_params=_params(2),
        name="merge",
    )(y_gelu, o_attn, proj, proj, w_glu, w_glu, w_ao)


def _outproj_router_kernel(m_ref, h_ref, wo_ref, g_ref, wr_ref, br_ref,
                           h1_ref, hn_ref, idx_ref, gate_ref, cnt_ref, *, tm, lp, seq, bsz):
    i = pl.program_id(0)
    n_e = wr_ref.shape[1]

    @pl.when(i == 0)
    def _():
        cnt_ref[...] = jnp.zeros_like(cnt_ref)

    h1 = h_ref[...] + jnp.dot(m_ref[...], wo_ref[...], preferred_element_type=F32)
    h1_ref[...] = h1
    hn = _rms(h1, g_ref[...])
    hn_ref[...] = hn
    logits = jnp.dot(hn, wr_ref[...], preferred_element_type=F32, precision=HIGHEST) + br_ref[...]

    row = i * tm + lax.broadcasted_iota(jnp.int32, (tm, 1), 0)
    valid = row < 0
    for b in range(bsz):
        valid = valid | ((row >= b * lp) & (row < b * lp + seq))

    lane = lax.broadcasted_iota(jnp.int32, (tm, n_e), 1).astype(F32)
    work = logits
    vals, idxs, hots = [], [], []
    for _ in range(TOP_K):
        mk = jnp.max(work, axis=-1, keepdims=True)
        ik = jnp.min(jnp.where(work == mk, lane, float(n_e)), axis=-1, keepdims=True)
        hot = lane == ik
        work = jnp.where(hot, -jnp.inf, work)
        vals.append(mk)
        idxs.append(ik)
        hots.append(hot)
    exps = [jnp.exp(v - vals[0]) for v in vals]
    denom = exps[0]
    for e in exps[1:]:
        denom = denom + e

    multi = jnp.zeros((tm, n_e), F32)
    for hot in hots:
        multi = multi + jnp.where(hot & valid, 1.0, 0.0)
    r_i = lax.broadcasted_iota(jnp.int32, (tm, tm), 0)
    c_i = lax.broadcasted_iota(jnp.int32, (tm, tm), 1)
    tri = jnp.where(r_i > c_i, 1.0, 0.0).astype(BF16)
    before = jnp.dot(tri, multi.astype(BF16), preferred_element_type=F32) + cnt_ref[0:1, 0:n_e]

    lane_o = lax.broadcasted_iota(jnp.int32, (tm, LANES), 1)
    idx_out = jnp.zeros((tm, LANES), jnp.int32)
    gate_out = jnp.zeros((tm, LANES), F32)
    for k in range(TOP_K):
        rank = jnp.sum(jnp.where(hots[k], before, 0.0), axis=-1, keepdims=True).astype(jnp.int32)
        idx_out = jnp.where(lane_o == k, idxs[k].astype(jnp.int32), idx_out)
        idx_out = jnp.where(lane_o == TOP_K + k, rank, idx_out)
        gate_out = jnp.where(lane_o == k, exps[k] / denom, gate_out)
    idx_ref[...] = idx_out
    gate_ref[...] = gate_out
    cnt_ref[0:1, 0:n_e] = cnt_ref[0:1, 0:n_e] + jnp.sum(multi, axis=0, keepdims=True)


def _outproj_router(merged, h, w_out, g, w_router, b_router, lp, seq, bsz):
    tp, d = h.shape
    n_e = w_router.shape[1]
    tm = _div_tile(tp, 256, LANES)
    row = lambda i: (i, 0)
    fixed = lambda i: (0, 0)
    return pl.pallas_call(
        functools.partial(_outproj_router_kernel, tm=tm, lp=lp, seq=seq, bsz=bsz),
        grid=(tp // tm,),
        in_specs=[pl.BlockSpec((tm, d), row), pl.BlockSpec((tm, d), row),
                  pl.BlockSpec((d, d), fixed), pl.BlockSpec((1, d), fixed),
                  pl.BlockSpec((d, n_e), fixed), pl.BlockSpec((1, n_e), fixed)],
        out_specs=[pl.BlockSpec((tm, d), row), pl.BlockSpec((tm, d), row),
                   pl.BlockSpec((tm, LANES), row), pl.BlockSpec((tm, LANES), row),
                   pl.BlockSpec((8, LANES), fixed)],
        out_shape=[jax.ShapeDtypeStruct((tp, d), F32), jax.ShapeDtypeStruct((tp, d), F32),
                   jax.ShapeDtypeStruct((tp, LANES), jnp.int32), jax.ShapeDtypeStruct((tp, LANES), F32),
                   jax.ShapeDtypeStruct((8, LANES), F32)],
        compiler_params=_params(1),
        name="outproj_router",
    )(merged, h, w_out, g, w_router, b_router)


def _row_copy(src_hbm, row, dst, dst_row, sem):
    return pltpu.make_async_copy(src_hbm.at[pl.ds(row, 1), :], dst.at[pl.ds(dst_row, 1), :], sem)


def _dispatch_kernel(slot_ref, hn_ref, xs_in, xs_hbm, sem, *, tm):
    del xs_in
    i = pl.program_id(0)

    def issue(r, c):
        base = (i * tm + r) * TOP_K
        for k in range(TOP_K):
            pltpu.make_async_copy(hn_ref.at[pl.ds(r, 1), :],
                                  xs_hbm.at[pl.ds(slot_ref[base + k], 1), :], sem).start()
        return c

    lax.fori_loop(0, tm, issue, 0, unroll=DMA_UNROLL)
    for _ in range(TOP_K):
        pltpu.make_async_copy(hn_ref, xs_hbm.at[pl.ds(0, tm), :], sem).wait()


def _dispatch(slot_flat, hn, n_rows):
    tp, d = hn.shape
    tm = _div_tile(tp, 256, LANES)
    return pl.pallas_call(
        functools.partial(_dispatch_kernel, tm=tm),
        grid_spec=pltpu.PrefetchScalarGridSpec(
            num_scalar_prefetch=1,
            grid=(tp // tm,),
            in_specs=[pl.BlockSpec((tm, d), lambda i, s: (i, 0)),
                      pl.BlockSpec(memory_space=pl.ANY)],
            out_specs=pl.BlockSpec(memory_space=pl.ANY),
            scratch_shapes=[pltpu.SemaphoreType.DMA]),
        out_shape=jax.ShapeDtypeStruct((n_rows, d), hn.dtype),
        input_output_aliases={2: 0},
        compiler_params=_params(1),
        name="dispatch",
    )(slot_flat, hn, jnp.zeros((n_rows, d), hn.dtype))


def _w1_prep_kernel(w_ref, o_ref):
    r = lax.broadcasted_iota(jnp.int32, (MXU_DIM, MXU_DIM), 0)
    c = lax.broadcasted_iota(jnp.int32, (MXU_DIM, MXU_DIM), 1)
    sel = jnp.where(r == jnp.where(c < LANES, 2 * c, 2 * (c - LANES) + 1), 1.0, 0.0).astype(BF16)
    for g in range(w_ref.shape[2] // MXU_DIM):
        cols = slice(g * MXU_DIM, (g + 1) * MXU_DIM)
        o_ref[0, :, cols] = jnp.dot(w_ref[0, :, cols].astype(BF16), sel,
                                    preferred_element_type=F32).astype(BF16)


def _w1_prep(w1):
    n_e, d, f2 = w1.shape
    tw = _div_tile(f2, 1024, MXU_DIM)
    spec = pl.BlockSpec((1, d, tw), lambda e, j: (e, 0, j))
    return pl.pallas_call(
        _w1_prep_kernel,
        grid=(n_e, f2 // tw),
        in_specs=[spec],
        out_specs=spec,
        out_shape=jax.ShapeDtypeStruct(w1.shape, BF16),
        compiler_params=_params(2),
        name="w1_prep",
    )(w1)


def _ffn1_kernel(be_ref, bm_ref, nused_ref, x_ref, w_ref, b_ref, o_ref):
    @pl.when(pl.program_id(1) < nused_ref[0])
    def _():
        h = jnp.dot(x_ref[...].astype(BF16), w_ref[0], preferred_element_type=F32) + b_ref[0]
        for g in range(h.shape[1] // MXU_DIM):
            glu = jnp.minimum(h[:, g * MXU_DIM:g * MXU_DIM + LANES], SWIGLU_LIMIT)
            lin = jnp.clip(h[:, g * MXU_DIM + LANES:(g + 1) * MXU_DIM], -SWIGLU_LIMIT, SWIGLU_LIMIT)
            o_ref[:, g * LANES:(g + 1) * LANES] = (
                glu * jax.nn.sigmoid(SWIGLU_ALPHA * glu) * (lin + 1.0)).astype(o_ref.dtype)

    @pl.when(pl.program_id(1) >= nused_ref[0])
    def _():
        o_ref[...] = jnp.zeros_like(o_ref)


def _ffn1(blk_expert, blk_map, n_used, xs, w1g, b1g, pb):
    n_slots = blk_map.shape[0] * pb
    d = xs.shape[1]
    f = w1g.shape[-1] // 2
    tn = _div_tile(f, 1024, LANES)
    return pl.pallas_call(
        _ffn1_kernel,
        grid_spec=pltpu.PrefetchScalarGridSpec(
            num_scalar_prefetch=3,
            grid=(f // tn, n_slots // pb),
            in_specs=[pl.BlockSpec((pb, d), lambda j, i, be, bm, nu: (bm[i], 0)),
                      pl.BlockSpec((1, d, 2 * tn), lambda j, i, be, bm, nu: (be[i], 0, j)),
                      pl.BlockSpec((1, 1, 2 * tn), lambda j, i, be, bm, nu: (be[i], 0, j))],
            out_specs=pl.BlockSpec((pb, tn), lambda j, i, be, bm, nu: (i, j))),
        out_shape=jax.ShapeDtypeStruct((n_slots, f), BF16),
        compiler_params=_params(2),
        name="ffn1",
    )(blk_expert, blk_map, n_used, xs, w1g, b1g)


def _ffn2_kernel(be_ref, bm_ref, nused_ref, a_ref, w_ref, b_ref, o_ref, wb_ref):
    i = pl.program_id(1)

    @pl.when((i == 0) | (be_ref[i] != be_ref[jnp.maximum(i - 1, 0)]))
    def _():
        wb_ref[...] = w_ref[0].astype(BF16)

    @pl.when(i < nused_ref[0])
    def _():
        o_ref[...] = jnp.dot(a_ref[...], wb_ref[...], preferred_element_type=F32) + b_ref[0]

    @pl.when(i >= nused_ref[0])
    def _():
        o_ref[...] = jnp.zeros_like(o_ref)


def _ffn2(blk_expert, blk_map, n_used, act, w2, b2, pb):
    n_slots, f = act.shape
    d = w2.shape[-1]
    tn = _div_tile(d, 1024, LANES)
    return pl.pallas_call(
        _ffn2_kernel,
        grid_spec=pltpu.PrefetchScalarGridSpec(
            num_scalar_prefetch=3,
            grid=(d // tn, n_slots // pb),
            in_specs=[pl.BlockSpec((pb, f), lambda j, i, be, bm, nu: (bm[i], 0)),
                      pl.BlockSpec((1, f, tn), lambda j, i, be, bm, nu: (be[i], 0, j)),
                      pl.BlockSpec((1, 1, tn), lambda j, i, be, bm, nu: (be[i], 0, j))],
            out_specs=pl.BlockSpec((pb, tn), lambda j, i, be, bm, nu: (i, j)),
            scratch_shapes=[pltpu.VMEM((f, tn), BF16)]),
        out_shape=jax.ShapeDtypeStruct((n_slots, d), F32),
        compiler_params=_params(2),
        name="ffn2",
    )(blk_expert, blk_map, n_used, act, w2, b2)


def _combine_kernel(slot_ref, ys_hbm, h1_ref, gate_ref, g_ref, o_ref, buf, sem, *, tm):
    i = pl.program_id(0)

    def issue(r, c):
        for k in range(TOP_K):
            _row_copy(ys_hbm, slot_ref[(i * tm + r) * TOP_K + k], buf.at[k], r, sem).start()
        return c

    lax.fori_loop(0, tm, issue, 0, unroll=DMA_UNROLL)
    for k in range(TOP_K):
        pltpu.make_async_copy(ys_hbm.at[pl.ds(0, tm), :], buf.at[k], sem).wait()
    gates = gate_ref[...]
    h2 = h1_ref[...]
    for k in range(TOP_K):
        h2 = h2 + gates[:, k:k + 1] * buf[k]
    o_ref[...] = _rms(h2, g_ref[...])


def _combine(slot_flat, ys, h1, gates, g):
    tp, d = h1.shape
    tm = _div_tile(tp, 128, LANES)
    row = lambda i, s: (i, 0)
    return pl.pallas_call(
        functools.partial(_combine_kernel, tm=tm),
        grid_spec=pltpu.PrefetchScalarGridSpec(
            num_scalar_prefetch=1,
            grid=(tp // tm,),
            in_specs=[pl.BlockSpec(memory_space=pl.ANY),
                      pl.BlockSpec((tm, d), row), pl.BlockSpec((tm, LANES), row),
                      pl.BlockSpec((1, d), lambda i, s: (0, 0))],
            out_specs=pl.BlockSpec((tm, d), row),
            scratch_shapes=[pltpu.VMEM((TOP_K, tm, d), F32), pltpu.SemaphoreType.DMA]),
        out_shape=jax.ShapeDtypeStruct((tp, d), F32),
        compiler_params=_params(1),
        name="combine",
    )(slot_flat, ys, h1, gates, g)


def kernel(x, meta_tokens, norm_mix_g, w_in, ssm_lam_re_f, ssm_lam_im_f, ssm_log_dt_f, ssm_b_re_f, ssm_b_im_f, ssm_c_re_f, ssm_c_im_f, ssm_lam_re_b, ssm_lam_im_b, ssm_log_dt_b, ssm_b_re_b, ssm_b_im_b, ssm_c_re_b, ssm_c_im_b, ssm_d, w_ssm_glu, diff_lam_q1, diff_lam_k1, diff_lam_q2, diff_lam_k2, diff_subln_g, w_attn_out, w_out, norm_moe_g, w_router, b_router, w_exp1, b_exp1, w_exp2, b_exp2, norm_final_g):
    bsz, s_len, d = x.shape
    depth = w_in.shape[0]
    assert depth == 1, "the combine kernel fuses the final RMSNorm, so exactly one layer is supported"
    seq = s_len + N_META
    lp = -(-seq // ATTN_TK) * ATTN_TK
    tp = bsz * lp
    ssm_w = ssm_d.shape[-1]
    g_n = ssm_w // SSM_GROUP
    qk_w = ATTN_HEADS * 2 * ATTN_HEAD_DIM
    v_w = ATTN_HEADS * ATTN_V_DIM
    n_e = w_router.shape[-1]
    f_ff = w_exp2.shape[2]
    tc = SSM_CHUNK
    nc = lp // tc
    assert 8 % bsz == 0, "the S5 scan packs batch x sequence segments into 8-row tiles"
    nseg = 8 // bsz
    assert nc % nseg == 0
    ncs = nc // nseg
    pb = MOE_BLOCK

    meta = jnp.broadcast_to(meta_tokens.astype(x.dtype)[None], (bsz, N_META, d))
    h = jnp.concatenate([meta, x, jnp.zeros((bsz, lp - seq, d), x.dtype)], axis=1).reshape(tp, d)

    pos = jnp.arange(tp, dtype=jnp.int32) % lp
    tok_valid = pos < seq
    slopes = jnp.exp2(-ALIBI_MAX_EXP * jnp.arange(1, ATTN_HEADS + 1, dtype=F32) / ATTN_HEADS)

    for layer in range(depth):
        proj = _inproj(h, norm_mix_g[layer][None], w_in[layer].astype(BF16))

        u_g = proj[:, :ssm_w].reshape(bsz, nseg, ncs, tc, g_n, SSM_GROUP)
        u_g = u_g.transpose(4, 2, 1, 0, 3, 5).reshape(g_n, nc * bsz, tc * SSM_GROUP)
        fw = (ssm_lam_re_f, ssm_lam_im_f, ssm_log_dt_f, ssm_b_re_f, ssm_b_im_f, ssm_c_re_f, ssm_c_im_f)
        bw = (ssm_lam_re_b, ssm_lam_im_b, ssm_log_dt_b, ssm_b_re_b, ssm_b_im_b, ssm_c_re_b, ssm_c_im_b)
        w1, cmat, decay = _ssm_operators([p[layer].astype(F32) for p in fw],
                                         [p[layer].astype(F32) for p in bw],
                                         ssm_d[layer].astype(F32), tc, ncs)
        y_g = _ssm(u_g, w1, cmat, decay, ncs, bsz)
        y_gelu = y_g.reshape(g_n, ncs, nseg, bsz, tc, SSM_GROUP).transpose(3, 2, 1, 4, 0, 5).reshape(tp, ssm_w)

        lam_init = 0.8 - 0.6 * math.exp(-0.3 * layer)
        lam = (jnp.exp(jnp.sum(diff_lam_q1[layer].astype(F32) * diff_lam_k1[layer].astype(F32)))
               - jnp.exp(jnp.sum(diff_lam_q2[layer].astype(F32) * diff_lam_k2[layer].astype(F32)))
               + lam_init)
        o_attn = _attention(proj, slopes, lam.reshape(1), diff_subln_g[layer][None].astype(F32),
                            bsz, lp, seq, ssm_w, lam_init)

        merged = _merge(y_gelu, o_attn, proj, w_ssm_glu[layer].astype(BF16),
                        w_attn_out[layer].astype(BF16), ssm_w + 2 * qk_w + v_w, d)

        h1, hn, ridx, rgate, rcnt = _outproj_router(
            merged, h, w_out[layer].astype(BF16), norm_moe_g[layer][None].astype(F32),
            w_router[layer].astype(F32), b_router[layer][None].astype(F32), lp, seq, bsz)

        e_idx = ridx[:, :TOP_K]
        rank = ridx[:, TOP_K:2 * TOP_K]
        counts = rcnt[0, :n_e].astype(jnp.int32)
        padded = ((counts + pb - 1) // pb) * pb
        pad_end = jnp.cumsum(padded)
        pad_start = pad_end - padded
        n_blocks = -(-(bsz * seq * TOP_K) // pb) + n_e
        n_slots = n_blocks * pb
        slot = pad_start[e_idx] + rank
        n_used = (pad_end[-1] // pb).astype(jnp.int32)
        blk_map = jnp.minimum(jnp.arange(n_blocks, dtype=jnp.int32), jnp.maximum(n_used - 1, 0))
        blk_expert = jnp.minimum(jnp.sum(pad_end[None, :] <= (blk_map * pb)[:, None], axis=1), n_e - 1).astype(jnp.int32)
        n_used1 = n_used.reshape(1)

        w1g = _w1_prep(w_exp1[layer].astype(F32))
        b1g = (b_exp1[layer].astype(F32).reshape(n_e, 2 * f_ff // MXU_DIM, LANES, 2)
               .transpose(0, 1, 3, 2).reshape(n_e, 1, 2 * f_ff))

        n_scratch = bsz * (lp - seq) * TOP_K
        scratch = n_slots + (jnp.cumsum(~tok_valid) - 1)[:, None] * TOP_K + jnp.arange(TOP_K)[None, :]
        xs = _dispatch(jnp.where(tok_valid[:, None], slot, scratch).astype(jnp.int32).reshape(-1),
                       hn, n_slots + n_scratch)
        act = _ffn1(blk_expert, blk_map, n_used1, xs, w1g, b1g, pb)
        ys = _ffn2(blk_expert, blk_map, n_used1, act, w_exp2[layer].astype(F32),
                   b_exp2[layer].astype(F32)[:, None, :], pb)

        slot_flat = jnp.where(tok_valid[:, None], slot, 0).reshape(-1)
        h = _combine(slot_flat, ys, h1, rgate, norm_final_g[None].astype(F32))
    return h.reshape(bsz, lp, d)[:, N_META:seq]
```

```python
import functools
import math

import jax
import jax.numpy as jnp
from jax import lax
from jax.experimental import pallas as pl
from jax.experimental.pallas import tpu as pltpu

F32 = jnp.float32
BF16 = jnp.bfloat16

N_META = 16
NORM_EPS = 1e-5
SSM_GROUP = 16
SSM_STATE = 64
ATTN_HEADS = 8
ATTN_HEAD_DIM = 64
ATTN_V_DIM = 2 * ATTN_HEAD_DIM
ALIBI_MAX_EXP = 8.0
ALIBI_SPLIT_BITS = 7
ALIBI_SPLIT = 1 << ALIBI_SPLIT_BITS
TOP_K = 4
SWIGLU_LIMIT = 7.0
SWIGLU_ALPHA = 1.702

LANES = 128
MXU_DIM = 256
ATTN_TK = 3 * MXU_DIM
ATTN_TQ = ATTN_TK
ATTN_UNROLL = 5
DMA_UNROLL = 8
SSM_CHUNK = 32
MOE_BLOCK = 512
VMEM_LIMIT = 56 * 1024 * 1024
NEG_BIG = -1e30
HIGHEST = lax.Precision.HIGHEST


def _div_tile(n, target, mult):
    best = None
    for t in range(mult, min(n, target) + 1, mult):
        if n % t == 0:
            best = t
    assert best is not None, (n, target, mult)
    return best


def _params(n_axes):
    return pltpu.CompilerParams(dimension_semantics=("arbitrary",) * n_axes,
                                vmem_limit_bytes=VMEM_LIMIT)


def _rms(x, g):
    return x * lax.rsqrt(jnp.mean(x * x, axis=-1, keepdims=True) + NORM_EPS) * g


def _inproj_kernel(h_ref, g_ref, w_ref, o_ref, xn_ref):
    @pl.when(pl.program_id(1) == 0)
    def _():
        xn_ref[...] = _rms(h_ref[...], g_ref[...]).astype(BF16)

    o_ref[...] = jnp.dot(xn_ref[...], w_ref[...], preferred_element_type=F32).astype(o_ref.dtype)


def _inproj(h, g, w_bf16):
    tp, d = h.shape
    n = w_bf16.shape[1]
    tm = _div_tile(tp, 768, LANES)
    tn = _div_tile(n, 1024, LANES)
    return pl.pallas_call(
        _inproj_kernel,
        grid=(tp // tm, n // tn),
        in_specs=[pl.BlockSpec((tm, d), lambda i, j: (i, 0)),
                  pl.BlockSpec((1, d), lambda i, j: (0, 0)),
                  pl.BlockSpec((d, tn), lambda i, j: (0, j))],
        out_specs=pl.BlockSpec((tm, tn), lambda i, j: (i, j)),
        out_shape=jax.ShapeDtypeStruct((tp, n), BF16),
        scratch_shapes=[pltpu.VMEM((tm, d), BF16)],
        compiler_params=_params(2),
        name="inproj",
    )(h, g, w_bf16)


def _ssm_direction_terms(lam_re, lam_im, log_dt, b_re, b_im):
    dt = jnp.exp(log_dt)[:, None]
    zr = lam_re * dt
    zi = lam_im * dt

    def apow(n):
        n = n.astype(F32)[:, None, None]
        mag = jnp.exp(n * zr[None])
        return mag * jnp.cos(n * zi[None]), mag * jnp.sin(n * zi[None])

    ar, ai = apow(jnp.ones((1,), F32))
    ar, ai = ar[0], ai[0]
    den = lam_re * lam_re + lam_im * lam_im
    fr = ((ar - 1.0) * lam_re + ai * lam_im) / den
    fi = (ai * lam_re - (ar - 1.0) * lam_im) / den
    bbr = fr[..., None] * b_re - fi[..., None] * b_im
    bbi = fr[..., None] * b_im + fi[..., None] * b_re
    return apow, bbr, bbi


def _pad_last(x, n):
    return jnp.pad(x, [(0, 0)] * (x.ndim - 1) + [(0, n - x.shape[-1])])


def _ssm_operators(fw, bw, ssm_d, tc, ncs):
    g_n, p_n = fw[0].shape
    io = SSM_GROUP
    n = jnp.arange(tc)
    terms = []
    for (lam_re, lam_im, log_dt, b_re, b_im, c_re, c_im), rev in ((fw, False), (bw, True)):
        apow, bbr, bbi = _ssm_direction_terms(lam_re, lam_im, log_dt, b_re, b_im)
        er, ei = apow(n)
        mr = c_re[None] * er[:, :, None, :] - c_im[None] * ei[:, :, None, :]
        mi = c_re[None] * ei[:, :, None, :] + c_im[None] * er[:, :, None, :]
        kern = (jnp.einsum('ngop,gpi->ngoi', mr, bbr, precision=HIGHEST)
                - jnp.einsum('ngop,gpi->ngoi', mi, bbi, precision=HIGHEST))
        es_r, es_i = (er, ei) if rev else (er[::-1], ei[::-1])
        st_re = es_r[..., None] * bbr[None] - es_i[..., None] * bbi[None]
        st_im = es_r[..., None] * bbi[None] + es_i[..., None] * bbr[None]
        st = [_pad_last(x.transpose(1, 0, 3, 2).reshape(g_n, tc * io, p_n), LANES)
              for x in (st_re, st_im)]
        cr_, ci_ = apow((tc - n) if rev else (n + 1))
        rr = c_re[None] * cr_[:, :, None, :] - c_im[None] * ci_[:, :, None, :]
        ri = c_re[None] * ci_[:, :, None, :] + c_im[None] * cr_[:, :, None, :]
        ro = [jnp.pad(x.transpose(1, 3, 0, 2).reshape(g_n, p_n, tc * io),
                      ((0, 0), (0, LANES - p_n), (0, 0))) for x in (rr, -ri)]
        dr, di = apow(jnp.array([tc, tc * ncs]))
        dec = [_pad_last(dr, LANES), _pad_last(di, LANES)]
        terms.append((kern, st, ro, dec))
    (kf, stf, rof, decf), (kb, stb, rob, decb) = terms
    dmat = jnp.eye(io, dtype=F32)[None] * ssm_d.reshape(g_n, io)[:, :, None]
    kcat = jnp.concatenate([kb[:0:-1], kf[:1] + kb[:1] + dmat[None], kf[1:]], axis=0)
    idx = (n[None, :] - n[:, None]) + (tc - 1)
    toep = kcat[idx]
    toep = toep.transpose(2, 0, 4, 1, 3).reshape(g_n, tc * io, tc * io)
    w1 = jnp.concatenate([toep] + stf + stb, axis=-1).astype(BF16)
    cmat = jnp.concatenate(rof + rob, axis=1).astype(BF16)
    chunk_seg = [jnp.stack([decf[0][k], decf[1][k], decb[0][k], decb[1][k]], axis=1) for k in (0, 1)]
    decay = jnp.concatenate(chunk_seg, axis=1)
    return w1, cmat, decay


def _gelu_tanh(y):
    return 0.5 * y * (1.0 + jnp.tanh(math.sqrt(2.0 / math.pi) * (y + 0.044715 * (y * y * y))))


def _cmul(ar, ai, xr, xi):
    return ar * xr - ai * xi, ar * xi + ai * xr


def _ssm_kernel(u_ref, w_ref, c_ref, dec_ref, o_ref, s_ref, h_ref, *, ncs, nb, n_toep):
    z = jnp.dot(u_ref[0], w_ref[0], preferred_element_type=F32)
    s_ref[...] = z[:, n_toep:]
    dec = dec_ref[0]
    afr, afi, abr, abi = dec[0:1], dec[1:2], dec[2:3], dec[3:4]
    sfr, sfi, sbr, sbi = dec[4:5], dec[5:6], dec[6:7], dec[7:8]
    lf_re, lf_im = slice(0, LANES), slice(LANES, 2 * LANES)
    lb_re, lb_im = slice(2 * LANES, 3 * LANES), slice(3 * LANES, 4 * LANES)

    def tiles(c):
        return (pl.ds(pl.multiple_of(c * 8, 8), 8), pl.ds(pl.multiple_of((ncs - 1 - c) * 8, 8), 8))

    def local(c, carry):
        hfr, hfi, hbr, hbi = carry
        rf, rb = tiles(c)
        h_ref[rf, lf_re] = hfr
        h_ref[rf, lf_im] = hfi
        h_ref[rb, lb_re] = hbr
        h_ref[rb, lb_im] = hbi
        nfr, nfi = _cmul(afr, afi, hfr, hfi)
        nbr, nbi = _cmul(abr, abi, hbr, hbi)
        return (nfr + s_ref[rf, lf_re], nfi + s_ref[rf, lf_im],
                nbr + s_ref[rb, lb_re], nbi + s_ref[rb, lb_im])

    zero = jnp.zeros((8, LANES), F32)
    efr, efi, ebr, ebi = lax.fori_loop(0, ncs, local, (zero, zero, zero, zero))

    row8 = lax.broadcasted_iota(jnp.int32, (8, LANES), 0)

    def shift_dn(x):
        return jnp.where(row8 >= nb, pltpu.roll(x, nb, 0), 0.0)

    def shift_up(x):
        return jnp.where(row8 < 8 - nb, pltpu.roll(x, 8 - nb, 0), 0.0)

    def segment_in(er, ei, ar, ai, shift):
        xr, xi = shift(er), shift(ei)
        ir, ii = xr, xi
        for _ in range(8 // nb - 2):
            tr, ti = _cmul(ar, ai, shift(ir), shift(ii))
            ir, ii = xr + tr, xi + ti
        return ir, ii

    ifr, ifi = segment_in(efr, efi, sfr, sfi, shift_dn)
    ibr, ibi = segment_in(ebr, ebi, sbr, sbi, shift_up)

    def fix(c, carry):
        pfr, pfi, pbr, pbi = carry
        rf, rb = tiles(c)
        dfr, dfi = _cmul(pfr, pfi, ifr, ifi)
        dbr, dbi = _cmul(pbr, pbi, ibr, ibi)
        h_ref[rf, lf_re] = h_ref[rf, lf_re] + dfr
        h_ref[rf, lf_im] = h_ref[rf, lf_im] + dfi
        h_ref[rb, lb_re] = h_ref[rb, lb_re] + dbr
        h_ref[rb, lb_im] = h_ref[rb, lb_im] + dbi
        return _cmul(afr, afi, pfr, pfi) + _cmul(abr, abi, pbr, pbi)

    one = jnp.ones((8, LANES), F32)
    lax.fori_loop(0, ncs, fix, (one, zero, one, zero))

    y = z[:, :n_toep] + jnp.dot(h_ref[...].astype(BF16), c_ref[0], preferred_element_type=F32)
    o_ref[0] = _gelu_tanh(y).astype(o_ref.dtype)


def _ssm(u_g, w1, cmat, decay, ncs, nb):
    g_n, rows, n_toep = u_g.shape
    n_w = w1.shape[-1]
    n_st = n_w - n_toep
    return pl.pallas_call(
        functools.partial(_ssm_kernel, ncs=ncs, nb=nb, n_toep=n_toep),
        grid=(g_n,),
        in_specs=[pl.BlockSpec((1, rows, n_toep), lambda g: (g, 0, 0)),
                  pl.BlockSpec((1, n_toep, n_w), lambda g: (g, 0, 0)),
                  pl.BlockSpec((1, n_st, n_toep), lambda g: (g, 0, 0)),
                  pl.BlockSpec((1, 8, LANES), lambda g: (g, 0, 0))],
        out_specs=pl.BlockSpec((1, rows, n_toep), lambda g: (g, 0, 0)),
        out_shape=jax.ShapeDtypeStruct((g_n, rows, n_toep), BF16),
        scratch_shapes=[pltpu.VMEM((rows, n_st), F32), pltpu.VMEM((rows, n_st), F32)],
        compiler_params=_params(1),
        name="ssm",
    )(u_g, w1, cmat, decay)


def _key_features(lp, seq):
    pos = jnp.arange(lp, dtype=jnp.int32)
    live = (pos >= N_META).astype(F32)
    cols = [live, live,
            (pos >> ALIBI_SPLIT_BITS).astype(F32) * live,
            (pos & (ALIBI_SPLIT - 1)).astype(F32) * live,
            (pos >= seq).astype(F32)]
    return _pad_last(jnp.stack(cols, axis=1), LANES).astype(BF16)


def _attn_kernel(slope_ref, lam_ref, q_ref, k_ref, v_ref, kf_ref, g_ref, o_ref, *, tq, tk, lam_init):
    slope = slope_ref[pl.program_id(1)]
    lam = lam_ref[0]
    q0 = pl.program_id(2) * tq
    nk = k_ref.shape[0] // tk
    q = q_ref[...] * jnp.asarray(ATTN_HEAD_DIM ** -0.5, BF16)
    lane = lax.broadcasted_iota(jnp.int32, q.shape, 1)
    zero = jnp.zeros_like(q)
    q_maps = (jnp.where(lane < ATTN_HEAD_DIM, q, zero),
              jnp.where(lane >= ATTN_HEAD_DIM, q, zero))

    qpos = q0 + lax.broadcasted_iota(jnp.int32, q.shape, 0)
    live = jnp.where(qpos >= N_META, 1.0, 0.0)
    hi = lax.shift_right_logical(qpos, ALIBI_SPLIT_BITS).astype(F32) * (slope * ALIBI_SPLIT) * live
    lo = (qpos & (ALIBI_SPLIT - 1)).astype(F32) * slope * live
    dist_cols = jnp.where(lane == 0, -hi, jnp.where(lane == 1, -lo, jnp.where(
        lane == 2, slope * ALIBI_SPLIT * live, jnp.where(lane == 3, slope * live, 0.0))))
    pad_col = jnp.where(lane == 4, NEG_BIG, 0.0)
    f_left = (dist_cols + pad_col).astype(BF16)
    f_right = (pad_col - dist_cols).astype(BF16)

    nt = (((1,), (1,)), ((), ()))
    ones = jnp.ones((tk, LANES), BF16)

    def step(j, carry, feats):
        rows = pl.ds(pl.multiple_of(j * tk, tk), tk)
        kx = jnp.concatenate([k_ref[rows, :], kf_ref[rows, :]], axis=1)
        vx = jnp.concatenate([v_ref[rows, :], ones], axis=1)
        out = []
        for mp in (0, 1):
            m, acc = carry[2 * mp], carry[2 * mp + 1]
            s = None
            for f in feats:
                sf = lax.dot_general(jnp.concatenate([q_maps[mp], f], axis=1), kx, nt,
                                     preferred_element_type=F32)
                s = sf if s is None else jnp.minimum(s, sf)
            m_new = jnp.maximum(m, jnp.max(s, axis=-1, keepdims=True))
            p = jnp.exp(s - m_new).astype(BF16)
            acc = jnp.exp(m - m_new) * acc + jnp.dot(p, vx, preferred_element_type=F32)
            out += [m_new, acc]
        return tuple(out)

    diag = q0 // tk

    def off_diag(j, carry):
        jj = jnp.where(j >= diag, j + 1, j)
        return step(jj, carry, (jnp.where(jj < diag, f_left, f_right),))

    m0 = jnp.full((tq, 1), NEG_BIG, F32)
    a0 = jnp.zeros((tq, 2 * LANES), F32)
    carry = lax.fori_loop(0, nk - 1, off_diag, (m0, a0, m0, a0), unroll=ATTN_UNROLL)
    _, a1, _, a2 = step(diag, carry, (f_left, f_right))
    o = (a1[:, :ATTN_V_DIM] / a1[:, ATTN_V_DIM:ATTN_V_DIM + 1]
         - lam * (a2[:, :ATTN_V_DIM] / a2[:, ATTN_V_DIM:ATTN_V_DIM + 1]))
    o_ref[...] = (_rms(o, g_ref[...]) * (1.0 - lam_init)).astype(o_ref.dtype)


def _attention(proj, slopes, lam, subln_g, bsz, lp, seq, col0, lam_init):
    tp = proj.shape[0]
    tk, tq = ATTN_TK, ATTN_TQ
    assert lp % tk == 0 and tk % tq == 0
    nq = lp // tq
    qc, kc, vc = (col0 // LANES + i * ATTN_HEADS for i in range(3))
    fixed = lambda b, h, i: (0, 0)
    return pl.pallas_call(
        functools.partial(_attn_kernel, tq=tq, tk=tk, lam_init=lam_init),
        grid=(bsz, ATTN_HEADS, nq),
        in_specs=[pl.BlockSpec(memory_space=pltpu.SMEM),
                  pl.BlockSpec(memory_space=pltpu.SMEM),
                  pl.BlockSpec((tq, LANES), lambda b, h, i: (b * nq + i, qc + h)),
                  pl.BlockSpec((lp, LANES), lambda b, h, i: (b, kc + h)),
                  pl.BlockSpec((lp, LANES), lambda b, h, i: (b, vc + h)),
                  pl.BlockSpec((lp, LANES), fixed),
                  pl.BlockSpec((1, LANES), fixed)],
        out_specs=pl.BlockSpec((tq, LANES), lambda b, h, i: (b * nq + i, h)),
        out_shape=jax.ShapeDtypeStruct((tp, ATTN_HEADS * ATTN_V_DIM), BF16),
        compiler_params=_params(3),
        name="attn",
    )(slopes, lam, proj, proj, proj, _key_features(lp, seq), subln_g)


def _merge_kernel(y_ref, o_ref, gs_ref, ga_ref, wa_ref, wb_ref, wo_ref, out_ref):
    y = y_ref[...]
    glu_a = jnp.dot(y, wa_ref[...], preferred_element_type=F32)
    glu_b = jnp.dot(y, wb_ref[...], preferred_element_type=F32)
    y_attn = jnp.dot(o_ref[...], wo_ref[...], preferred_element_type=F32)
    y_ssm = glu_a * jax.nn.sigmoid(glu_b)
    merged = (jax.nn.sigmoid(gs_ref[...].astype(F32)) * y_ssm
              + jax.nn.sigmoid(ga_ref[...].astype(F32)) * y_attn)
    out_ref[...] = merged.astype(out_ref.dtype)


def _merge(y_gelu, o_attn, proj, w_glu, w_ao, gate_col0, d):
    tp, w = y_gelu.shape
    tm = _div_tile(tp, 640, LANES)
    tn = _div_tile(d, 512, LANES)
    nj = d // tn
    gs0 = gate_col0 // tn
    return pl.pallas_call(
        _merge_kernel,
        grid=(tp // tm, nj),
        in_specs=[pl.BlockSpec((tm, w), lambda i, j: (i, 0)),
                  pl.BlockSpec((tm, o_attn.shape[1]), lambda i, j: (i, 0)),
                  pl.BlockSpec((tm, tn), lambda i, j: (i, gs0 + j)),
                  pl.BlockSpec((tm, tn), lambda i, j: (i, gs0 + nj + j)),
                  pl.BlockSpec((w, tn), lambda i, j: (0, j)),
                  pl.BlockSpec((w, tn), lambda i, j: (0, nj + j)),
                  pl.BlockSpec((w_ao.shape[0], tn), lambda i, j: (0, j))],
        out_specs=pl.BlockSpec((tm, tn), lambda i, j: (i, j)),
        out_shape=jax.ShapeDtypeStruct((tp, d), BF16),
        compiler_params=_params(2),
        name="merge",
    )(y_gelu, o_attn, proj, proj, w_glu, w_glu, w_ao)


def _outproj_router_kernel(m_ref, h_ref, wo_ref, g_ref, wr_ref, br_ref,
                           h1_ref, hn_ref, idx_ref, gate_ref, cnt_ref, *, tm, lp, seq, bsz):
    i = pl.program_id(0)
    n_e = wr_ref.shape[1]

    @pl.when(i == 0)
    def _():
        cnt_ref[...] = jnp.zeros_like(cnt_ref)

    h1 = h_ref[...] + jnp.dot(m_ref[...], wo_ref[...], preferred_element_type=F32)
    h1_ref[...] = h1
    hn = _rms(h1, g_ref[...])
    hn_ref[...] = hn
    logits = jnp.dot(hn, wr_ref[...], preferred_element_type=F32, precision=HIGHEST) + br_ref[...]

    row = i * tm + lax.broadcasted_iota(jnp.int32, (tm, 1), 0)
    valid = row < 0
    for b in range(bsz):
        valid = valid | ((row >= b * lp) & (row < b * lp + seq))

    lane = lax.broadcasted_iota(jnp.int32, (tm, n_e), 1).astype(F32)
    work = logits
    vals, idxs, hots = [], [], []
    for _ in range(TOP_K):
        mk = jnp.max(work, axis=-1, keepdims=True)
        ik = jnp.min(jnp.where(work == mk, lane, float(n_e)), axis=-1, keepdims=True)
        hot = lane == ik
        work = jnp.where(hot, -jnp.inf, work)
        vals.append(mk)
        idxs.append(ik)
        hots.append(hot)
    exps = [jnp.exp(v - vals[0]) for v in vals]
    denom = exps[0]
    for e in exps[1:]:
        denom = denom + e

    multi = jnp.zeros((tm, n_e), F32)
    for hot in hots:
        multi = multi + jnp.where(hot & valid, 1.0, 0.0)
    r_i = lax.broadcasted_iota(jnp.int32, (tm, tm), 0)
    c_i = lax.broadcasted_iota(jnp.int32, (tm, tm), 1)
    tri = jnp.where(r_i > c_i, 1.0, 0.0).astype(BF16)
    before = jnp.dot(tri, multi.astype(BF16), preferred_element_type=F32) + cnt_ref[0:1, 0:n_e]

    lane_o = lax.broadcasted_iota(jnp.int32, (tm, LANES), 1)
    idx_out = jnp.zeros((tm, LANES), jnp.int32)
    gate_out = jnp.zeros((tm, LANES), F32)
    for k in range(TOP_K):
        rank = jnp.sum(jnp.where(hots[k], before, 0.0), axis=-1, keepdims=True).astype(jnp.int32)
        idx_out = jnp.where(lane_o == k, idxs[k].astype(jnp.int32), idx_out)
        idx_out = jnp.where(lane_o == TOP_K + k, rank, idx_out)
        gate_out = jnp.where(lane_o == k, exps[k] / denom, gate_out)
    idx_ref[...] = idx_out
    gate_ref[...] = gate_out
    cnt_ref[0:1, 0:n_e] = cnt_ref[0:1, 0:n_e] + jnp.sum(multi, axis=0, keepdims=True)


def _outproj_router(merged, h, w_out, g, w_router, b_router, lp, seq, bsz):
    tp, d = h.shape
    n_e = w_router.shape[1]
    tm = _div_tile(tp, 256, LANES)
    row = lambda i: (i, 0)
    fixed = lambda i: (0, 0)
    return pl.pallas_call(
        functools.partial(_outproj_router_kernel, tm=tm, lp=lp, seq=seq, bsz=bsz),
        grid=(tp // tm,),
        in_specs=[pl.BlockSpec((tm, d), row), pl.BlockSpec((tm, d), row),
                  pl.BlockSpec((d, d), fixed), pl.BlockSpec((1, d), fixed),
                  pl.BlockSpec((d, n_e), fixed), pl.BlockSpec((1, n_e), fixed)],
        out_specs=[pl.BlockSpec((tm, d), row), pl.BlockSpec((tm, d), row),
                   pl.BlockSpec((tm, LANES), row), pl.BlockSpec((tm, LANES), row),
                   pl.BlockSpec((8, LANES), fixed)],
        out_shape=[jax.ShapeDtypeStruct((tp, d), F32), jax.ShapeDtypeStruct((tp, d), F32),
                   jax.ShapeDtypeStruct((tp, LANES), jnp.int32), jax.ShapeDtypeStruct((tp, LANES), F32),
                   jax.ShapeDtypeStruct((8, LANES), F32)],
        compiler_params=_params(1),
        name="outproj_router",
    )(merged, h, w_out, g, w_router, b_router)


def _row_copy(src_hbm, row, dst, dst_row, sem):
    return pltpu.make_async_copy(src_hbm.at[pl.ds(row, 1), :], dst.at[pl.ds(dst_row, 1), :], sem)


def _dispatch_kernel(slot_ref, hn_ref, xs_in, xs_hbm, sem, *, tm):
    del xs_in
    i = pl.program_id(0)

    def issue(r, c):
        base = (i * tm + r) * TOP_K
        for k in range(TOP_K):
            pltpu.make_async_copy(hn_ref.at[pl.ds(r, 1), :],
                                  xs_hbm.at[pl.ds(slot_ref[base + k], 1), :], sem).start()
        return c

    lax.fori_loop(0, tm, issue, 0, unroll=DMA_UNROLL)
    for _ in range(TOP_K):
        pltpu.make_async_copy(hn_ref, xs_hbm.at[pl.ds(0, tm), :], sem).wait()


def _dispatch(slot_flat, hn, n_rows):
    tp, d = hn.shape
    tm = _div_tile(tp, 256, LANES)
    return pl.pallas_call(
        functools.partial(_dispatch_kernel, tm=tm),
        grid_spec=pltpu.PrefetchScalarGridSpec(
            num_scalar_prefetch=1,
            grid=(tp // tm,),
            in_specs=[pl.BlockSpec((tm, d), lambda i, s: (i, 0)),
                      pl.BlockSpec(memory_space=pl.ANY)],
            out_specs=pl.BlockSpec(memory_space=pl.ANY),
            scratch_shapes=[pltpu.SemaphoreType.DMA]),
        out_shape=jax.ShapeDtypeStruct((n_rows, d), hn.dtype),
        input_output_aliases={2: 0},
        compiler_params=_params(1),
        name="dispatch",
    )(slot_flat, hn, jnp.zeros((n_rows, d), hn.dtype))


def _w1_prep_kernel(w_ref, o_ref):
    r = lax.broadcasted_iota(jnp.int32, (MXU_DIM, MXU_DIM), 0)
    c = lax.broadcasted_iota(jnp.int32, (MXU_DIM, MXU_DIM), 1)
    sel = jnp.where(r == jnp.where(c < LANES, 2 * c, 2 * (c - LANES) + 1), 1.0, 0.0).astype(BF16)
    for g in range(w_ref.shape[2] // MXU_DIM):
        cols = slice(g * MXU_DIM, (g + 1) * MXU_DIM)
        o_ref[0, :, cols] = jnp.dot(w_ref[0, :, cols].astype(BF16), sel,
                                    preferred_element_type=F32).astype(BF16)


def _w1_prep(w1):
    n_e, d, f2 = w1.shape
    tw = _div_tile(f2, 1024, MXU_DIM)
    spec = pl.BlockSpec((1, d, tw), lambda e, j: (e, 0, j))
    return pl.pallas_call(
        _w1_prep_kernel,
        grid=(n_e, f2 // tw),
        in_specs=[spec],
        out_specs=spec,
        out_shape=jax.ShapeDtypeStruct(w1.shape, BF16),
        compiler_params=_params(2),
        name="w1_prep",
    )(w1)


def _ffn1_kernel(be_ref, bm_ref, nused_ref, x_ref, w_ref, b_ref, o_ref):
    @pl.when(pl.program_id(1) < nused_ref[0])
    def _():
        h = jnp.dot(x_ref[...].astype(BF16), w_ref[0], preferred_element_type=F32) + b_ref[0]
        for g in range(h.shape[1] // MXU_DIM):
            glu = jnp.minimum(h[:, g * MXU_DIM:g * MXU_DIM + LANES], SWIGLU_LIMIT)
            lin = jnp.clip(h[:, g * MXU_DIM + LANES:(g + 1) * MXU_DIM], -SWIGLU_LIMIT, SWIGLU_LIMIT)
            o_ref[:, g * LANES:(g + 1) * LANES] = (
                glu * jax.nn.sigmoid(SWIGLU_ALPHA * glu) * (lin + 1.0)).astype(o_ref.dtype)

    @pl.when(pl.program_id(1) >= nused_ref[0])
    def _():
        o_ref[...] = jnp.zeros_like(o_ref)


def _ffn1(blk_expert, blk_map, n_used, xs, w1g, b1g, pb):
    n_slots = blk_map.shape[0] * pb
    d = xs.shape[1]
    f = w1g.shape[-1] // 2
    tn = _div_tile(f, 1024, LANES)
    return pl.pallas_call(
        _ffn1_kernel,
        grid_spec=pltpu.PrefetchScalarGridSpec(
            num_scalar_prefetch=3,
            grid=(f // tn, n_slots // pb),
            in_specs=[pl.BlockSpec((pb, d), lambda j, i, be, bm, nu: (bm[i], 0)),
                      pl.BlockSpec((1, d, 2 * tn), lambda j, i, be, bm, nu: (be[i], 0, j)),
                      pl.BlockSpec((1, 1, 2 * tn), lambda j, i, be, bm, nu: (be[i], 0, j))],
            out_specs=pl.BlockSpec((pb, tn), lambda j, i, be, bm, nu: (i, j))),
        out_shape=jax.ShapeDtypeStruct((n_slots, f), BF16),
        compiler_params=_params(2),
        name="ffn1",
    )(blk_expert, blk_map, n_used, xs, w1g, b1g)


def _ffn2_kernel(be_ref, bm_ref, nused_ref, a_ref, w_ref, b_ref, o_ref, wb_ref):
    i = pl.program_id(1)

    @pl.when((i == 0) | (be_ref[i] != be_ref[jnp.maximum(i - 1, 0)]))
    def _():
        wb_ref[...] = w_ref[0].astype(BF16)

    @pl.when(i < nused_ref[0])
    def _():
        o_ref[...] = jnp.dot(a_ref[...], wb_ref[...], preferred_element_type=F32) + b_ref[0]

    @pl.when(i >= nused_ref[0])
    def _():
        o_ref[...] = jnp.zeros_like(o_ref)


def _ffn2(blk_expert, blk_map, n_used, act, w2, b2, pb):
    n_slots, f = act.shape
    d = w2.shape[-1]
    tn = _div_tile(d, 1024, LANES)
    return pl.pallas_call(
        _ffn2_kernel,
        grid_spec=pltpu.PrefetchScalarGridSpec(
            num_scalar_prefetch=3,
            grid=(d // tn, n_slots // pb),
            in_specs=[pl.BlockSpec((pb, f), lambda j, i, be, bm, nu: (bm[i], 0)),
                      pl.BlockSpec((1, f, tn), lambda j, i, be, bm, nu: (be[i], 0, j)),
                      pl.BlockSpec((1, 1, tn), lambda j, i, be, bm, nu: (be[i], 0, j))],
            out_specs=pl.BlockSpec((pb, tn), lambda j, i, be, bm, nu: (i, j)),
            scratch_shapes=[pltpu.VMEM((f, tn), BF16)]),
        out_shape=jax.ShapeDtypeStruct((n_slots, d), F32),
        compiler_params=_params(2),
        name="ffn2",
    )(blk_expert, blk_map, n_used, act, w2, b2)


def _combine_kernel(slot_ref, ys_hbm, h1_ref, gate_ref, g_ref, o_ref, buf, sem, *, tm):
    i = pl.program_id(0)

    def issue(r, c):
        for k in range(TOP_K):
            _row_copy(ys_hbm, slot_ref[(i * tm + r) * TOP_K + k], buf.at[k], r, sem).start()
        return c

    lax.fori_loop(0, tm, issue, 0, unroll=DMA_UNROLL)
    for k in range(TOP_K):
        pltpu.make_async_copy(ys_hbm.at[pl.ds(0, tm), :], buf.at[k], sem).wait()
    gates = gate_ref[...]
    h2 = h1_ref[...]
    for k in range(TOP_K):
        h2 = h2 + gates[:, k:k + 1] * buf[k]
    o_ref[...] = _rms(h2, g_ref[...])


def _combine(slot_flat, ys, h1, gates, g):
    tp, d = h1.shape
    tm = _div_tile(tp, 256, LANES)
    row = lambda i, s: (i, 0)
    return pl.pallas_call(
        functools.partial(_combine_kernel, tm=tm),
        grid_spec=pltpu.PrefetchScalarGridSpec(
            num_scalar_prefetch=1,
            grid=(tp // tm,),
            in_specs=[pl.BlockSpec(memory_space=pl.ANY),
                      pl.BlockSpec((tm, d), row), pl.BlockSpec((tm, LANES), row),
                      pl.BlockSpec((1, d), lambda i, s: (0, 0))],
            out_specs=pl.BlockSpec((tm, d), row),
            scratch_shapes=[pltpu.VMEM((TOP_K, tm, d), F32), pltpu.SemaphoreType.DMA]),
        out_shape=jax.ShapeDtypeStruct((tp, d), F32),
        compiler_params=_params(1),
        name="combine",
    )(slot_flat, ys, h1, gates, g)


def kernel(x, meta_tokens, norm_mix_g, w_in, ssm_lam_re_f, ssm_lam_im_f, ssm_log_dt_f, ssm_b_re_f, ssm_b_im_f, ssm_c_re_f, ssm_c_im_f, ssm_lam_re_b, ssm_lam_im_b, ssm_log_dt_b, ssm_b_re_b, ssm_b_im_b, ssm_c_re_b, ssm_c_im_b, ssm_d, w_ssm_glu, diff_lam_q1, diff_lam_k1, diff_lam_q2, diff_lam_k2, diff_subln_g, w_attn_out, w_out, norm_moe_g, w_router, b_router, w_exp1, b_exp1, w_exp2, b_exp2, norm_final_g):
    bsz, s_len, d = x.shape
    depth = w_in.shape[0]
    assert depth == 1, "the combine kernel fuses the final RMSNorm, so exactly one layer is supported"
    seq = s_len + N_META
    lp = -(-seq // ATTN_TK) * ATTN_TK
    tp = bsz * lp
    ssm_w = ssm_d.shape[-1]
    g_n = ssm_w // SSM_GROUP
    qk_w = ATTN_HEADS * 2 * ATTN_HEAD_DIM
    v_w = ATTN_HEADS * ATTN_V_DIM
    n_e = w_router.shape[-1]
    f_ff = w_exp2.shape[2]
    tc = SSM_CHUNK
    nc = lp // tc
    assert 8 % bsz == 0, "the S5 scan packs batch x sequence segments into 8-row tiles"
    nseg = 8 // bsz
    assert nc % nseg == 0
    ncs = nc // nseg
    pb = MOE_BLOCK

    meta = jnp.broadcast_to(meta_tokens.astype(x.dtype)[None], (bsz, N_META, d))
    h = jnp.concatenate([meta, x, jnp.zeros((bsz, lp - seq, d), x.dtype)], axis=1).reshape(tp, d)

    pos = jnp.arange(tp, dtype=jnp.int32) % lp
    tok_valid = pos < seq
    slopes = jnp.exp2(-ALIBI_MAX_EXP * jnp.arange(1, ATTN_HEADS + 1, dtype=F32) / ATTN_HEADS)

    for layer in range(depth):
        proj = _inproj(h, norm_mix_g[layer][None], w_in[layer].astype(BF16))

        u_g = proj[:, :ssm_w].reshape(bsz, nseg, ncs, tc, g_n, SSM_GROUP)
        u_g = u_g.transpose(4, 2, 1, 0, 3, 5).reshape(g_n, nc * bsz, tc * SSM_GROUP)
        fw = (ssm_lam_re_f, ssm_lam_im_f, ssm_log_dt_f, ssm_b_re_f, ssm_b_im_f, ssm_c_re_f, ssm_c_im_f)
        bw = (ssm_lam_re_b, ssm_lam_im_b, ssm_log_dt_b, ssm_b_re_b, ssm_b_im_b, ssm_c_re_b, ssm_c_im_b)
        w1, cmat, decay = _ssm_operators([p[layer].astype(F32) for p in fw],
                                         [p[layer].astype(F32) for p in bw],
                                         ssm_d[layer].astype(F32), tc, ncs)
        y_g = _ssm(u_g, w1, cmat, decay, ncs, bsz)
        y_gelu = y_g.reshape(g_n, ncs, nseg, bsz, tc, SSM_GROUP).transpose(3, 2, 1, 4, 0, 5).reshape(tp, ssm_w)

        lam_init = 0.8 - 0.6 * math.exp(-0.3 * layer)
        lam = (jnp.exp(jnp.sum(diff_lam_q1[layer].astype(F32) * diff_lam_k1[layer].astype(F32)))
               - jnp.exp(jnp.sum(diff_lam_q2[layer].astype(F32) * diff_lam_k2[layer].astype(F32)))
               + lam_init)
        o_attn = _attention(proj, slopes, lam.reshape(1), diff_subln_g[layer][None].astype(F32),
                            bsz, lp, seq, ssm_w, lam_init)

        merged = _merge(y_gelu, o_attn, proj, w_ssm_glu[layer].astype(BF16),
                        w_attn_out[layer].astype(BF16), ssm_w + 2 * qk_w + v_w, d)

        h1, hn, ridx, rgate, rcnt = _outproj_router(
            merged, h, w_out[layer].astype(BF16), norm_moe_g[layer][None].astype(F32),
            w_router[layer].astype(F32), b_router[layer][None].astype(F32), lp, seq, bsz)

        e_idx = ridx[:, :TOP_K]
        rank = ridx[:, TOP_K:2 * TOP_K]
        counts = rcnt[0, :n_e].astype(jnp.int32)
        padded = ((counts + pb - 1) // pb) * pb
        pad_end = jnp.cumsum(padded)
        pad_start = pad_end - padded
        n_blocks = -(-(bsz * seq * TOP_K) // pb) + n_e
        n_slots = n_blocks * pb
        slot = pad_start[e_idx] + rank
        n_used = (pad_end[-1] // pb).astype(jnp.int32)
        blk_map = jnp.minimum(jnp.arange(n_blocks, dtype=jnp.int32), jnp.maximum(n_used - 1, 0))
        blk_expert = jnp.minimum(jnp.sum(pad_end[None, :] <= (blk_map * pb)[:, None], axis=1), n_e - 1).astype(jnp.int32)
        n_used1 = n_used.reshape(1)

        w1g = _w1_prep(w_exp1[layer].astype(F32))
        b1g = (b_exp1[layer].astype(F32).reshape(n_e, 2 * f_ff // MXU_DIM, LANES, 2)
               .transpose(0, 1, 3, 2).reshape(n_e, 1, 2 * f_ff))

        n_scratch = bsz * (lp - seq) * TOP_K
        scratch = n_slots + (jnp.cumsum(~tok_valid) - 1)[:, None] * TOP_K + jnp.arange(TOP_K)[None, :]
        xs = _dispatch(jnp.where(tok_valid[:, None], slot, scratch).astype(jnp.int32).reshape(-1),
                       hn, n_slots + n_scratch)
        act = _ffn1(blk_expert, blk_map, n_used1, xs, w1g, b1g, pb)
        ys = _ffn2(blk_expert, blk_map, n_used1, act, w_exp2[layer].astype(F32),
                   b_exp2[layer].astype(F32)[:, None, :], pb)

        slot_flat = jnp.where(tok_valid[:, None], slot, 0).reshape(-1)
        h = _combine(slot_flat, ys, h1, rgate, norm_final_g[None].astype(F32))
    return h.reshape(bsz, lp, d)[:, N_META:seq]
```
